```python
import math
import jax, jax.numpy as jnp
from jax import lax
import numpy as np

D_MODEL = 1024
BATCH = 8
SEQ = 4096
DEPTH = 1
DEC_BATCH = 8
DEC_SEQ = 2048
PAST_LEN = 128

GLA_HEADS = 4
GLA_DK = D_MODEL // 16
GLA_DV = D_MODEL // 8
GLA_KEY = GLA_HEADS * GLA_DK
GLA_VAL = GLA_HEADS * GLA_DV
GATE_RANK = 16
GATE_NORM = 16.0
GLA_CHUNK = 64
SG_GROUPS = 4
SG_CH = D_MODEL // 8
SG_WIDTH = SG_GROUPS * SG_CH
SG_CHUNK = 128
MIX_WIDTH = GLA_VAL + SG_WIDTH
IN_SIZES = (GLA_KEY, GLA_KEY, GLA_VAL, GLA_VAL, GATE_RANK, GATE_RANK, SG_WIDTH, SG_WIDTH)
IN_COLS = sum(IN_SIZES)
IN_SPLITS = tuple(int(s) for s in np.cumsum(IN_SIZES)[:-1])
N_EXPERTS = 32
TOP_K = 4
D_FF = D_MODEL
SWIGLU_LIMIT = 7.0
SWIGLU_ALPHA = 1.702
EPS = 1e-5

kernel_name = "hymba_gla_gmlp_moe_encoder"


def rmsnorm(x, w):
    xf = x.astype(jnp.float32)
    xf = xf * lax.rsqrt(jnp.mean(xf * xf, axis=-1, keepdims=True) + EPS)
    return (xf * w.astype(jnp.float32)).astype(x.dtype)


def gla_chunked(q, k, v, g, strict):
    B, H, L, dk = q.shape
    dv = v.shape[-1]
    nc = L // GLA_CHUNK

    def chunks(t):
        return jnp.moveaxis(t.reshape(B, H, nc, GLA_CHUNK, t.shape[-1]), 2, 0)

    mask = jnp.tril(jnp.ones((GLA_CHUNK, GLA_CHUNK), jnp.float32), k=-1 if strict else 0)

    def step(S, inp):
        qc, kc, vc, gc = inp
        b = jnp.cumsum(gc, axis=-2)
        qt = qc * jnp.exp(b)
        kt = kc * jnp.exp(-b)
        A = jnp.einsum("bhid,bhjd->bhij", qt, kt) * mask
        o = jnp.einsum("bhij,bhjv->bhiv", A, vc) + jnp.einsum("bhid,bhdv->bhiv", qt, S)
        bl = b[:, :, -1:, :]
        S = S * jnp.exp(bl[:, :, 0, :])[..., None] + jnp.einsum(
            "bhjd,bhjv->bhdv", kc * jnp.exp(bl - b), vc)
        return S, o

    S0 = jnp.zeros((B, H, dk, dv), jnp.float32)
    _, o = lax.scan(step, S0, (chunks(q), chunks(k), chunks(v), chunks(g)))
    return jnp.moveaxis(o, 0, 2).reshape(B, H, L, dv)


def token_mixer(xn, w_in, gla_w2_f, gla_b_f, gla_w2_b, gla_b_b, gla_norm_w,
                sg_norm_w, sg_norm_b, sg_w, sg_b, w_out):
    B, L, _ = xn.shape
    z = xn @ w_in
    q, k, v, r, gf, gb, u, vs = jnp.split(z, IN_SPLITS, axis=-1)

    def heads(t, d):
        return t.reshape(B, L, GLA_HEADS, d).transpose(0, 2, 1, 3).astype(jnp.float32)

    qh = heads(q, GLA_DK) * (GLA_DK ** -0.5)
    kh = heads(k, GLA_DK)
    vh = heads(v, GLA_DV)
    g_f = heads(jax.nn.log_sigmoid(gf @ gla_w2_f + gla_b_f) / GATE_NORM, GLA_DK)
    g_b = heads(jax.nn.log_sigmoid(gb @ gla_w2_b + gla_b_b) / GATE_NORM, GLA_DK)
    o_fwd = gla_chunked(qh, kh, vh, g_f, strict=False)
    flip = lambda t: jnp.flip(t, axis=2)
    o_bwd = flip(gla_chunked(flip(qh), flip(kh), flip(vh), flip(g_b), strict=True))
    o = (o_fwd + o_bwd).transpose(0, 2, 1, 3)
    o = o * lax.rsqrt(jnp.mean(o * o, axis=-1, keepdims=True) + EPS)
    o = o * gla_norm_w.reshape(GLA_HEADS, GLA_DV).astype(jnp.float32)
    gla_out = o.reshape(B, L, GLA_VAL).astype(xn.dtype) * jax.nn.silu(r)

    u = jax.nn.gelu(u, approximate=False)
    vs = jax.nn.gelu(vs, approximate=False).reshape(B, L, SG_GROUPS, SG_CH)
    vf = vs.astype(jnp.float32)
    mu = jnp.mean(vf, axis=-1, keepdims=True)
    var = jnp.mean(jnp.square(vf - mu), axis=-1, keepdims=True)
    vn = ((vf - mu) * lax.rsqrt(var + EPS)).astype(xn.dtype)
    vn = vn * sg_norm_w.reshape(SG_GROUPS, SG_CH) + sg_norm_b.reshape(SG_GROUPS, SG_CH)
    vn = vn.reshape(B, L // SG_CHUNK, SG_CHUNK, SG_GROUPS, SG_CH)
    s = jnp.einsum("gij,bnjgc->bnigc", sg_w, vn) + sg_b.T[None, None, :, :, None]
    sg_out = u * s.reshape(B, L, SG_WIDTH)

    mixed = jnp.concatenate([gla_out, sg_out], axis=-1)
    return mixed @ w_out


def moe(h, router_w, router_b, exp_w1, exp_b1, exp_w2, exp_b2):
    T = h.shape[0]
    logits = (h @ router_w + router_b).astype(jnp.float32)
    top_v, top_i = lax.top_k(logits, TOP_K)
    top_w = jax.nn.softmax(top_v, axis=-1)
    gates = jnp.sum(jax.nn.one_hot(top_i, N_EXPERTS, dtype=jnp.float32) * top_w[..., None], axis=1)
    gates = gates.astype(h.dtype).T

    def expert(acc, p):
        w1, b1, w2, b2, ge = p
        zz = h @ w1 + b1
        gate, lin = zz[:, :D_FF], zz[:, D_FF:]
        gate = jnp.minimum(gate, SWIGLU_LIMIT)
        lin = jnp.clip(lin, -SWIGLU_LIMIT, SWIGLU_LIMIT)
        a = gate * jax.nn.sigmoid(SWIGLU_ALPHA * gate) * (lin + 1.0)
        return acc + ge[:, None] * (a @ w2 + b2), None

    out, _ = lax.scan(expert, jnp.zeros_like(h), (exp_w1, exp_b1, exp_w2, exp_b2, gates))
    return out


def encoder(x, norm1_w, w_in, gla_w2_f, gla_b_f, gla_w2_b, gla_b_b, gla_norm_w,
            sg_norm_w, sg_norm_b, sg_w, sg_b, w_out, norm2_w, router_w, router_b,
            exp_w1, exp_b1, exp_w2, exp_b2, final_norm_w):
    B, L, D = x.shape
    h = x
    for _ in range(DEPTH):
        h = h + token_mixer(rmsnorm(h, norm1_w), w_in, gla_w2_f, gla_b_f, gla_w2_b, gla_b_b,
                            gla_norm_w, sg_norm_w, sg_norm_b, sg_w, sg_b, w_out)
        hf = h.reshape(B * L, D)
        h = h + moe(rmsnorm(hf, norm2_w), router_w, router_b,
                    exp_w1, exp_b1, exp_w2, exp_b2).reshape(B, L, D)
    return rmsnorm(h, final_norm_w)


def setup_inputs(seed: int = 0) -> dict:
    key = jax.random.key(seed)
    ks = jax.random.split(key, 24)
    nrm = lambda k, shape, scale: jax.random.normal(k, shape, jnp.float32) * scale
    return {
        "x_prompt": nrm(ks[0], (BATCH, SEQ, D_MODEL), 1.0),
        "x_sample": nrm(ks[1], (DEC_BATCH, DEC_SEQ, D_MODEL), 1.0),
        "norm1_w": 1.0 + nrm(ks[2], (D_MODEL,), 0.02),
        "w_in": nrm(ks[3], (D_MODEL, IN_COLS), D_MODEL ** -0.5),
        "gla_w2_f": nrm(ks[4], (GATE_RANK, GLA_KEY), GATE_RANK ** -0.5),
        "gla_b_f": nrm(ks[5], (GLA_KEY,), 0.1),
        "gla_w2_b": nrm(ks[6], (GATE_RANK, GLA_KEY), GATE_RANK ** -0.5),
        "gla_b_b": nrm(ks[7], (GLA_KEY,), 0.1),
        "gla_norm_w": 1.0 + nrm(ks[8], (GLA_VAL,), 0.02),
        "sg_norm_w": 1.0 + nrm(ks[9], (SG_WIDTH,), 0.02),
        "sg_norm_b": nrm(ks[10], (SG_WIDTH,), 0.02),
        "sg_w": nrm(ks[11], (SG_GROUPS, SG_CHUNK, SG_CHUNK), SG_CHUNK ** -0.5),
        "sg_b": 1.0 + nrm(ks[12], (SG_GROUPS, SG_CHUNK), 0.02),
        "w_out": nrm(ks[13], (MIX_WIDTH, D_MODEL), MIX_WIDTH ** -0.5),
        "norm2_w": 1.0 + nrm(ks[14], (D_MODEL,), 0.02),
        "router_w": nrm(ks[15], (D_MODEL, N_EXPERTS), D_MODEL ** -0.5),
        "router_b": nrm(ks[16], (N_EXPERTS,), 0.01),
        "exp_w1": nrm(ks[17], (N_EXPERTS, D_MODEL, 2 * D_FF), D_MODEL ** -0.5),
        "exp_b1": nrm(ks[18], (N_EXPERTS, 2 * D_FF), 0.02),
        "exp_w2": nrm(ks[19], (N_EXPERTS, D_FF, D_MODEL), D_FF ** -0.5),
        "exp_b2": nrm(ks[20], (N_EXPERTS, D_MODEL), 0.02),
        "final_norm_w": 1.0 + nrm(ks[21], (D_MODEL,), 0.02),
    }


def reference(x_prompt, x_sample, norm1_w, w_in, gla_w2_f, gla_b_f, gla_w2_b, gla_b_b,
              gla_norm_w, sg_norm_w, sg_norm_b, sg_w, sg_b, w_out, norm2_w, router_w,
              router_b, exp_w1, exp_b1, exp_w2, exp_b2, final_norm_w):
    y_prompt = encoder(x_prompt, norm1_w, w_in, gla_w2_f, gla_b_f, gla_w2_b, gla_b_b,
                       gla_norm_w, sg_norm_w, sg_norm_b, sg_w, sg_b, w_out, norm2_w,
                       router_w, router_b, exp_w1, exp_b1, exp_w2, exp_b2, final_norm_w)
    y_sample = encoder(x_sample, norm1_w, w_in, gla_w2_f, gla_b_f, gla_w2_b, gla_b_b,
                       gla_norm_w, sg_norm_w, sg_norm_b, sg_w, sg_b, w_out, norm2_w,
                       router_w, router_b, exp_w1, exp_b1, exp_w2, exp_b2, final_norm_w)
    return (y_prompt, y_sample)
```

```python
import functools

import jax
import jax.numpy as jnp
from jax import lax
from jax.experimental import pallas as pl
from jax.experimental.pallas import tpu as pltpu

D_MODEL = 1024
GLA_HEADS = 4
GLA_DK = 64
GLA_DV = 128
GATE_RANK = 16
GATE_NORM = 16.0
GLA_CHUNK = 64
SG_GROUPS = 4
SG_CH = 128
SG_CHUNK = 128
N_EXPERTS = 32
TOP_K = 4
D_FF = 1024
SWIGLU_LIMIT = 7.0
SWIGLU_ALPHA = 1.702
EPS = 1e-5

LANES = 128
VMEM_LIMIT = 56 * 1024 * 1024

HEAD_W = GLA_HEADS * LANES
ZC_Q, ZC_K, ZC_V, ZC_R, ZC_U, ZC_VS = 0, 1, 2, 3, 4, 5
ZC_GATE_BLK = 6 * HEAD_W // LANES
ZC = 6 * HEAD_W + LANES

TM = 512
GLA_G = 256
TB = 256
TM_E = 512

NEG = -1e30


def _cparams(sem):
    return pltpu.CompilerParams(dimension_semantics=sem, vmem_limit_bytes=VMEM_LIMIT)


def _inproj_kernel(np_tiles, xp_ref, xs_ref, n1_ref, w_ref, z_ref):
    i = pl.program_id(0)
    x = jnp.where(i < np_tiles, xp_ref[...], xs_ref[...])
    ms = jnp.mean(x * x, axis=-1, keepdims=True)
    xn = (x * lax.rsqrt(ms + EPS)) * n1_ref[...]
    z = jnp.dot(xn.astype(jnp.bfloat16), w_ref[...], preferred_element_type=jnp.float32)
    z_ref[...] = z.astype(jnp.bfloat16)


def _inproj(xp, xs, n1, w_all):
    tp, ts = xp.shape[0], xs.shape[0]
    npt, nst = tp // TM, ts // TM
    return pl.pallas_call(
        functools.partial(_inproj_kernel, npt),
        grid=(npt + nst,),
        in_specs=[
            pl.BlockSpec((TM, D_MODEL), lambda i: (jnp.minimum(i, npt - 1), 0)),
            pl.BlockSpec((TM, D_MODEL), lambda i: (jnp.maximum(i - npt, 0), 0)),
            pl.BlockSpec((1, D_MODEL), lambda i: (0, 0)),
            pl.BlockSpec((D_MODEL, ZC), lambda i: (0, 0)),
        ],
        out_specs=pl.BlockSpec((TM, ZC), lambda i: (i, 0)),
        out_shape=jax.ShapeDtypeStruct((tp + ts, ZC), jnp.bfloat16),
        compiler_params=_cparams(("arbitrary",)),
        name="inproj",
    )(xp, xs, n1, w_all)


def _log_sigmoid(x):
    return jnp.minimum(x, 0.0) - jnp.log1p(jnp.exp(-jnp.abs(x)))


_NT = (((1,), (1,)), ((), ()))
_TN = (((0,), (0,)), ((), ()))


def _gla_direction(q_ref, k_ref, v_ref, gz_ref, w2_ref, gb_ref, tri_ref, mask_ref,
                   st_ref, o_ref, reverse):
    nchunk = GLA_G // GLA_CHUNK
    gpre = jnp.dot(gz_ref[...], w2_ref[...], preferred_element_type=jnp.float32) + gb_ref[...]
    g = _log_sigmoid(gpre) * (1.0 / GATE_NORM)
    g_hi = g.astype(jnp.bfloat16)
    r1 = g - g_hi.astype(jnp.float32)
    g_mid = r1.astype(jnp.bfloat16)
    g_lo = (r1 - g_mid.astype(jnp.float32)).astype(jnp.bfloat16)
    tri = tri_ref[...]
    b = (jnp.dot(tri, g_hi, preferred_element_type=jnp.float32)
         + jnp.dot(tri, g_mid, preferred_element_type=jnp.float32)
         + jnp.dot(tri, g_lo, preferred_element_type=jnp.float32))
    mask = mask_ref[...]
    order = range(nchunk - 1, -1, -1) if reverse else range(nchunk)
    for c in order:
        r0 = c * GLA_CHUNK
        last = r0 if reverse else r0 + GLA_CHUNK - 1
        for h in range(GLA_HEADS):
            l0 = h * LANES
            bh = b[r0:r0 + GLA_CHUNK, l0:l0 + LANES]
            bl = b[last:last + 1, l0:l0 + LANES]
            qh = q_ref[r0:r0 + GLA_CHUNK, l0:l0 + LANES].astype(jnp.float32)
            kh = k_ref[r0:r0 + GLA_CHUNK, l0:l0 + LANES].astype(jnp.float32)
            vh = v_ref[r0:r0 + GLA_CHUNK, l0:l0 + LANES]
            qt = (qh * jnp.exp(bh)).astype(jnp.bfloat16)
            kt = (kh * jnp.exp(-bh)).astype(jnp.bfloat16)
            kd = (kh * jnp.exp(bl - bh)).astype(jnp.bfloat16)
            a = lax.dot_general(qt, kt, _NT, preferred_element_type=jnp.float32) * mask
            st = st_ref[h]
            o = (jnp.dot(a.astype(jnp.bfloat16), vh, preferred_element_type=jnp.float32)
                 + lax.dot_general(qt, st.astype(jnp.bfloat16), _NT,
                                   preferred_element_type=jnp.float32))
            o_ref[r0:r0 + GLA_CHUNK, l0:l0 + LANES] = o
            st_ref[h] = st * jnp.exp(bl) + lax.dot_general(
                vh, kd, _TN, preferred_element_type=jnp.float32)


def _gla_kernel(start_ref, n_ref,
                qf, kf, vf, gzf, qb, kb, vb, gzb,
                w2f, gbf, w2b, gbb, tril, triu, maskf, maskb,
                of_ref, ob_ref, stf, stb):
    s = pl.program_id(0)
    i = pl.program_id(1)

    @pl.when(i == 0)
    def _():
        stf[...] = jnp.zeros_like(stf)
        stb[...] = jnp.zeros_like(stb)

    @pl.when(i < n_ref[s])
    def _():
        _gla_direction(qf, kf, vf, gzf, w2f, gbf, tril, maskf, stf, of_ref, False)
        _gla_direction(qb, kb, vb, gzb, w2b, gbb, triu, maskb, stb, ob_ref, True)


def _gla(z, seq_start, seq_n, max_groups, w2f, gbf, w2b, gbb):
    t = z.shape[0]
    nseq = seq_start.shape[0]

    def fwd(s, i, st, n):
        return st[s] + jnp.minimum(i, n[s] - 1)

    def bwd(s, i, st, n):
        return st[s] + jnp.maximum(n[s] - 1 - i, 0)

    def zspec(width, col, blk):
        return pl.BlockSpec((GLA_G, width), lambda s, i, st, n: (blk(s, i, st, n), col))

    def const(shape):
        return pl.BlockSpec(shape, lambda s, i, st, n: (0,) * len(shape))

    r = lax.broadcasted_iota(jnp.int32, (GLA_G, GLA_G), 0)
    c = lax.broadcasted_iota(jnp.int32, (GLA_G, GLA_G), 1)
    same = (r // GLA_CHUNK) == (c // GLA_CHUNK)
    tril = (same & (c <= r)).astype(jnp.bfloat16)
    triu = (same & (c >= r)).astype(jnp.bfloat16)
    rr = lax.broadcasted_iota(jnp.int32, (GLA_CHUNK, GLA_CHUNK), 0)
    cc = lax.broadcasted_iota(jnp.int32, (GLA_CHUNK, GLA_CHUNK), 1)
    maskf = (cc <= rr).astype(jnp.float32)
    maskb = (cc > rr).astype(jnp.float32)

    in_specs = [
        zspec(HEAD_W, ZC_Q, fwd), zspec(HEAD_W, ZC_K, fwd), zspec(HEAD_W, ZC_V, fwd),
        zspec(LANES, ZC_GATE_BLK, fwd),
        zspec(HEAD_W, ZC_Q, bwd), zspec(HEAD_W, ZC_K, bwd), zspec(HEAD_W, ZC_V, bwd),
        zspec(LANES, ZC_GATE_BLK, bwd),
        const((LANES, HEAD_W)), const((1, HEAD_W)), const((LANES, HEAD_W)), const((1, HEAD_W)),
        const((GLA_G, GLA_G)), const((GLA_G, GLA_G)),
        const((GLA_CHUNK, GLA_CHUNK)), const((GLA_CHUNK, GLA_CHUNK)),
    ]
    out_specs = [
        pl.BlockSpec((GLA_G, HEAD_W), lambda s, i, st, n: (fwd(s, i, st, n), 0)),
        pl.BlockSpec((GLA_G, HEAD_W), lambda s, i, st, n: (bwd(s, i, st, n), 0)),
    ]
    grid_spec = pltpu.PrefetchScalarGridSpec(
        num_scalar_prefetch=2,
        grid=(nseq, max_groups),
        in_specs=in_specs,
        out_specs=out_specs,
        scratch_shapes=[pltpu.VMEM((GLA_HEADS, GLA_DV, LANES), jnp.float32),
                        pltpu.VMEM((GLA_HEADS, GLA_DV, LANES), jnp.float32)],
    )
    return pl.pallas_call(
        _gla_kernel,
        grid_spec=grid_spec,
        out_shape=[jax.ShapeDtypeStruct((t, HEAD_W), jnp.float32),
                   jax.ShapeDtypeStruct((t, HEAD_W), jnp.float32)],
        compiler_params=_cparams(("arbitrary", "arbitrary")),
        name="gla",
    )(seq_start, seq_n, z, z, z, z, z, z, z, z, w2f, gbf, w2b, gbb, tril, triu, maskf, maskb)


def _gelu(x):
    return 0.5 * x * (1.0 + lax.erf(x * (2.0 ** -0.5)))


def _mixer_router_kernel(np_tiles, of_ref, ob_ref, r_ref, u_ref, vs_ref, xp_ref, xs_ref,
                         gnw_ref, snw_ref, snb_ref, sgw_ref, sgb_ref, wout_ref, n2_ref,
                         rw_ref, rb_ref, tri_ref,
                         h_ref, hn_ref, ri_ref, rwt_ref, cnt_ref, carry_ref):
    i = pl.program_id(0)

    @pl.when(i == 0)
    def _():
        carry_ref[...] = jnp.zeros_like(carry_ref)

    o = of_ref[...] + ob_ref[...]
    parts = []
    for hd in range(GLA_HEADS):
        oh = o[:, hd * GLA_DV:(hd + 1) * GLA_DV]
        ms = jnp.mean(oh * oh, axis=-1, keepdims=True)
        parts.append(oh * lax.rsqrt(ms + EPS))
    on = jnp.concatenate(parts, axis=1) * gnw_ref[...]
    r = r_ref[...].astype(jnp.float32)
    gla_out = on * (r / (1.0 + jnp.exp(-r)))

    u = _gelu(u_ref[...].astype(jnp.float32))
    vf = _gelu(vs_ref[...].astype(jnp.float32))
    parts = []
    for g in range(SG_GROUPS):
        vg = vf[:, g * SG_CH:(g + 1) * SG_CH]
        mu = jnp.mean(vg, axis=-1, keepdims=True)
        vc = vg - mu
        var = jnp.mean(vc * vc, axis=-1, keepdims=True)
        parts.append(vc * lax.rsqrt(var + EPS))
    vn = (jnp.concatenate(parts, axis=1) * snw_ref[...] + snb_ref[...]).astype(jnp.bfloat16)
    nch = TM // SG_CHUNK
    res = []
    for g in range(SG_GROUPS):
        vcat = jnp.concatenate(
            [vn[c * SG_CHUNK:(c + 1) * SG_CHUNK, g * SG_CH:(g + 1) * SG_CH] for c in range(nch)],
            axis=1)
        res.append(jnp.dot(sgw_ref[g], vcat, preferred_element_type=jnp.float32))
    bias = sgb_ref[...]
    rows = []
    for c in range(nch):
        rows.append(jnp.concatenate(
            [res[g][:, c * SG_CH:(c + 1) * SG_CH] for g in range(SG_GROUPS)], axis=1) + bias)
    sg_out = u * jnp.concatenate(rows, axis=0)

    mixed = jnp.concatenate([gla_out, sg_out], axis=1).astype(jnp.bfloat16)
    x = jnp.where(i < np_tiles, xp_ref[...], xs_ref[...])
    h = x + jnp.dot(mixed, wout_ref[...], preferred_element_type=jnp.float32)
    h_ref[...] = h

    ms = jnp.mean(h * h, axis=-1, keepdims=True)
    hn = (h * lax.rsqrt(ms + EPS)) * n2_ref[...]
    hn_ref[...] = hn
    logits = jnp.dot(hn, rw_ref[...], preferred_element_type=jnp.float32,
                     precision=lax.Precision.HIGHEST) + rb_ref[...]
    lane = lax.broadcasted_iota(jnp.int32, logits.shape, 1)
    work = jnp.where(lane < N_EXPERTS, logits, NEG)
    top_v, top_i, onehots = [], [], []
    for _ in range(TOP_K):
        m = jnp.max(work, axis=-1, keepdims=True)
        idx = jnp.min(jnp.where(work == m, lane, LANES), axis=-1, keepdims=True)
        sel = lane == idx
        top_v.append(m)
        top_i.append(idx)
        onehots.append(sel)
        work = jnp.where(sel, NEG, work)
    e = [jnp.exp(v - top_v[0]) for v in top_v]
    den = e[0] + e[1] + e[2] + e[3]
    sel_any = onehots[0] | onehots[1] | onehots[2] | onehots[3]
    msel = sel_any.astype(jnp.bfloat16)
    before = jnp.dot(tri_ref[...], msel, preferred_element_type=jnp.float32) + carry_ref[...]
    ri = jnp.zeros(logits.shape, jnp.int32)
    rwt = jnp.zeros(logits.shape, jnp.float32)
    for k in range(TOP_K):
        rank = jnp.sum(jnp.where(onehots[k], before, 0.0), axis=-1, keepdims=True)
        ri = jnp.where(lane == k, top_i[k], ri)
        ri = jnp.where(lane == TOP_K + k, rank.astype(jnp.int32), ri)
        rwt = jnp.where(lane == k, e[k] / den, rwt)
    ri_ref[...] = ri
    rwt_ref[...] = rwt
    total = carry_ref[...] + jnp.sum(sel_any.astype(jnp.float32), axis=0, keepdims=True)
    carry_ref[...] = total
    cnt_ref[...] = total


def _mixer_router(o_f, o_b, z, xp, xs, gnw, snw, snb, sgw, sgb_full, wout, n2, rw, rb):
    t = z.shape[0]
    npt = xp.shape[0] // TM
    nt = t // TM
    tri = (lax.broadcasted_iota(jnp.int32, (TM, TM), 1)
           < lax.broadcasted_iota(jnp.int32, (TM, TM), 0)).astype(jnp.bfloat16)

    def const(shape):
        return pl.BlockSpec(shape, lambda i: (0,) * len(shape))

    def tok(width, col=0):
        return pl.BlockSpec((TM, width), lambda i: (i, col))

    in_specs = [
        tok(HEAD_W), tok(HEAD_W),
        tok(HEAD_W, ZC_R), tok(HEAD_W, ZC_U), tok(HEAD_W, ZC_VS),
        pl.BlockSpec((TM, D_MODEL), lambda i: (jnp.minimum(i, npt - 1), 0)),
        pl.BlockSpec((TM, D_MODEL), lambda i: (jnp.maximum(i - npt, 0), 0)),
        const((1, HEAD_W)), const((1, HEAD_W)), const((1, HEAD_W)),
        const((SG_GROUPS, SG_CHUNK, SG_CHUNK)), const((SG_CHUNK, HEAD_W)),
        const((D_MODEL, D_MODEL)), const((1, D_MODEL)),
        const((D_MODEL, LANES)), const((1, LANES)), const((TM, TM)),
    ]
    out_specs = [
        tok(D_MODEL), tok(D_MODEL), tok(LANES), tok(LANES), const((1, LANES)),
    ]
    out_shape = [
        jax.ShapeDtypeStruct((t, D_MODEL), jnp.float32),
        jax.ShapeDtypeStruct((t, D_MODEL), jnp.float32),
        jax.ShapeDtypeStruct((t, LANES), jnp.int32),
        jax.ShapeDtypeStruct((t, LANES), jnp.float32),
        jax.ShapeDtypeStruct((1, LANES), jnp.float32),
    ]
    return pl.pallas_call(
        functools.partial(_mixer_router_kernel, npt),
        grid=(nt,),
        in_specs=in_specs,
        out_specs=out_specs,
        out_shape=out_shape,
        scratch_shapes=[pltpu.VMEM((1, LANES), jnp.float32)],
        compiler_params=_cparams(("arbitrary",)),
        name="mixer_router",
    )(o_f, o_b, z, z, z, xp, xs, gnw, snw, snb, sgw, sgb_full, wout, n2, rw, rb, tri)


def _row_copy(src_ref, src_row, dst_ref, dst_row, sem):
    return pltpu.make_async_copy(src_ref.at[pl.ds(src_row, 1)], dst_ref.at[pl.ds(dst_row, 1)], sem)


def _dispatch_kernel(pos_ref, hn_ref, xs_ref, sem):
    def issue(t, carry):
        for k in range(TOP_K):
            _row_copy(hn_ref, t, xs_ref, pos_ref[TOP_K * t + k], sem.at[0]).start()
        return carry

    lax.fori_loop(0, TB, issue, 0, unroll=8)

    for _ in range(TOP_K):
        pltpu.make_async_copy(hn_ref, xs_ref.at[pl.ds(0, TB)], sem.at[0]).wait()


def _dispatch(hn, pos, p_rows):
    t = hn.shape[0]
    return pl.pallas_call(
        _dispatch_kernel,
        grid=(t // TB,),
        in_specs=[
            pl.BlockSpec((TOP_K * TB,), lambda i: (i,), memory_space=pltpu.SMEM),
            pl.BlockSpec((TB, D_MODEL), lambda i: (i, 0)),
        ],
        out_specs=pl.BlockSpec(memory_space=pl.ANY),
        out_shape=jax.ShapeDtypeStruct((p_rows, D_MODEL), jnp.float32),
        scratch_shapes=[pltpu.SemaphoreType.DMA((1,))],
        compiler_params=_cparams(("arbitrary",)),
        name="dispatch",
    )(pos, hn)


def _expert_kernel(te_ref, blk_ref, valid_ref, x_ref, w1_ref, b1_ref, w2_ref, b2_ref,
                   y_ref, w1b, w2b):
    j = pl.program_id(0)
    prev = te_ref[jnp.maximum(j - 1, 0)]

    @pl.when((j == 0) | (te_ref[j] != prev))
    def _():
        w1b[...] = w1_ref[0].astype(jnp.bfloat16)
        w2b[...] = w2_ref[0].astype(jnp.bfloat16)

    nvalid = valid_ref[j]

    @pl.when(nvalid == 0)
    def _():
        y_ref[...] = jnp.zeros_like(y_ref)

    @pl.when(nvalid > 0)
    def _():
        row = lax.broadcasted_iota(jnp.int32, (TM_E, 1), 0)
        x = jnp.where(row < nvalid, x_ref[...], 0.0).astype(jnp.bfloat16)
        zz = jnp.dot(x, w1b[...], preferred_element_type=jnp.float32) + b1_ref[0]
        gate = jnp.minimum(zz[:, :D_FF], SWIGLU_LIMIT)
        lin = jnp.clip(zz[:, D_FF:], -SWIGLU_LIMIT, SWIGLU_LIMIT)
        a = gate * (1.0 / (1.0 + jnp.exp(-SWIGLU_ALPHA * gate))) * (lin + 1.0)
        y_ref[...] = jnp.dot(a.astype(jnp.bfloat16), w2b[...],
                             preferred_element_type=jnp.float32) + b2_ref[0]


def _experts(x_sorted, te, blk, valid, w1, b1, w2, b2):
    p_rows = x_sorted.shape[0]
    ntiles = p_rows // TM_E
    grid_spec = pltpu.PrefetchScalarGridSpec(
        num_scalar_prefetch=3,
        grid=(ntiles,),
        in_specs=[
            pl.BlockSpec((TM_E, D_MODEL), lambda j, te, blk, v: (blk[j], 0)),
            pl.BlockSpec((1, D_MODEL, 2 * D_FF), lambda j, te, blk, v: (te[j], 0, 0)),
            pl.BlockSpec((1, 1, 2 * D_FF), lambda j, te, blk, v: (te[j], 0, 0)),
            pl.BlockSpec((1, D_FF, D_MODEL), lambda j, te, blk, v: (te[j], 0, 0)),
            pl.BlockSpec((1, 1, D_MODEL), lambda j, te, blk, v: (te[j], 0, 0)),
        ],
        out_specs=pl.BlockSpec((TM_E, D_MODEL), lambda j, te, blk, v: (j, 0)),
        scratch_shapes=[pltpu.VMEM((D_MODEL, 2 * D_FF), jnp.bfloat16),
                        pltpu.VMEM((D_FF, D_MODEL), jnp.bfloat16)],
    )
    return pl.pallas_call(
        _expert_kernel,
        grid_spec=grid_spec,
        out_shape=jax.ShapeDtypeStruct((p_rows, D_MODEL), jnp.float32),
        compiler_params=_cparams(("arbitrary",)),
        name="experts",
    )(te, blk, valid, x_sorted, w1, b1, w2, b2)


def _combine_kernel(np_tiles, pos_ref, h_ref, rwt_ref, fw_ref, ys_ref,
                    outp_ref, outs_ref, ybuf, sem):
    i = pl.program_id(0)

    def issue(t, carry):
        for k in range(TOP_K):
            _row_copy(ys_ref, pos_ref[TOP_K * t + k], ybuf.at[k], t, sem.at[0]).start()
        return carry

    lax.fori_loop(0, TB, issue, 0, unroll=8)

    for k in range(TOP_K):
        pltpu.make_async_copy(ys_ref.at[pl.ds(0, TB)], ybuf.at[k], sem.at[0]).wait()

    acc = h_ref[...]
    w = rwt_ref[...]
    for k in range(TOP_K):
        acc = acc + w[:, k:k + 1] * ybuf[k]
    ms = jnp.mean(acc * acc, axis=-1, keepdims=True)
    res = (acc * lax.rsqrt(ms + EPS)) * fw_ref[...]

    @pl.when(i < np_tiles)
    def _():
        outp_ref[...] = res

    @pl.when(i >= np_tiles)
    def _():
        outs_ref[...] = res


def _combine(h, rwt, pos, y_sorted, fw, tp, ts):
    t = h.shape[0]
    npt = tp // TB
    return pl.pallas_call(
        functools.partial(_combine_kernel, npt),
        grid=(t // TB,),
        in_specs=[
            pl.BlockSpec((TOP_K * TB,), lambda i: (i,), memory_space=pltpu.SMEM),
            pl.BlockSpec((TB, D_MODEL), lambda i: (i, 0)),
            pl.BlockSpec((TB, LANES), lambda i: (i, 0)),
            pl.BlockSpec((1, D_MODEL), lambda i: (0, 0)),
            pl.BlockSpec(memory_space=pl.ANY),
        ],
        out_specs=[
            pl.BlockSpec((TB, D_MODEL), lambda i: (jnp.minimum(i, npt - 1), 0)),
            pl.BlockSpec((TB, D_MODEL), lambda i: (jnp.maximum(i - npt, 0), 0)),
        ],
        out_shape=[jax.ShapeDtypeStruct((tp, D_MODEL), jnp.float32),
                   jax.ShapeDtypeStruct((ts, D_MODEL), jnp.float32)],
        scratch_shapes=[pltpu.VMEM((TOP_K, TB, D_MODEL), jnp.float32),
                        pltpu.SemaphoreType.DMA((1,))],
        compiler_params=_cparams(("arbitrary",)),
        name="combine",
    )(pos, h, rwt, fw, y_sorted)


def _pad_heads(w):
    rows = w.shape[0]
    w = w.reshape(rows, GLA_HEADS, GLA_DK)
    w = jnp.pad(w, ((0, 0), (0, 0), (0, LANES - GLA_DK)))
    return w.reshape(rows, HEAD_W)


def _forward(x_prompt, x_sample, norm1_w, w_in, gla_w2_f, gla_b_f, gla_w2_b, gla_b_b,
             gla_norm_w, sg_norm_w, sg_norm_b, sg_w, sg_b, w_out, norm2_w, router_w,
             router_b, exp_w1, exp_b1, exp_w2, exp_b2, final_norm_w):
    bp, lp, _ = x_prompt.shape
    bs, ls, _ = x_sample.shape
    tp, ts = bp * lp, bs * ls
    t = tp + ts
    assert lp % TM == 0 and ls % TM == 0 and TM % GLA_G == 0 and TM % TB == 0
    xp = x_prompt.reshape(tp, D_MODEL)
    xs = x_sample.reshape(ts, D_MODEL)

    k0 = GLA_HEADS * GLA_DK
    v0 = 2 * k0
    r0 = v0 + HEAD_W
    g0 = r0 + HEAD_W
    u0 = g0 + 2 * GATE_RANK
    wq = _pad_heads(w_in[:, :k0] * (GLA_DK ** -0.5))
    wk = _pad_heads(w_in[:, k0:v0])
    wgate = jnp.pad(w_in[:, g0:u0], ((0, 0), (0, LANES - 2 * GATE_RANK)))
    w_all = jnp.concatenate(
        [wq, wk, w_in[:, v0:r0], w_in[:, r0:g0], w_in[:, u0:], wgate], axis=1
    ).astype(jnp.bfloat16)

    z = _inproj(xp, xs, norm1_w.reshape(1, D_MODEL), w_all)

    w2f = jnp.zeros((LANES, HEAD_W), jnp.float32).at[:GATE_RANK].set(_pad_heads(gla_w2_f))
    w2b = jnp.zeros((LANES, HEAD_W), jnp.float32).at[GATE_RANK:2 * GATE_RANK].set(
        _pad_heads(gla_w2_b))
    gbf = _pad_heads(gla_b_f.reshape(1, -1))
    gbb = _pad_heads(gla_b_b.reshape(1, -1))
    gp, gs = lp // GLA_G, ls // GLA_G
    seq_start = jnp.concatenate([jnp.arange(bp, dtype=jnp.int32) * gp,
                                 bp * gp + jnp.arange(bs, dtype=jnp.int32) * gs])
    seq_n = jnp.concatenate([jnp.full((bp,), gp, jnp.int32), jnp.full((bs,), gs, jnp.int32)])
    o_f, o_b = _gla(z, seq_start, seq_n, max(gp, gs), w2f.astype(jnp.bfloat16), gbf,
                    w2b.astype(jnp.bfloat16), gbb)

    sgb_full = jnp.repeat(sg_b.T, SG_CH, axis=1)
    rw = jnp.pad(router_w, ((0, 0), (0, LANES - N_EXPERTS)))
    rb = jnp.pad(router_b.reshape(1, -1), ((0, 0), (0, LANES - N_EXPERTS)))
    h, hn, ri, rwt, cnt = _mixer_router(
        o_f, o_b, z, xp, xs, gla_norm_w.reshape(1, -1), sg_norm_w.reshape(1, -1),
        sg_norm_b.reshape(1, -1), sg_w.astype(jnp.bfloat16), sgb_full,
        w_out.astype(jnp.bfloat16), norm2_w.reshape(1, -1), rw, rb)

    counts = cnt[0, :N_EXPERTS].astype(jnp.int32)
    padded = ((counts + TM_E - 1) // TM_E) * TM_E
    gend = jnp.cumsum(padded)
    gstart = gend - padded
    pos = (gstart[ri[:, :TOP_K]] + ri[:, TOP_K:2 * TOP_K]).reshape(-1)
    ntiles = (TOP_K * t) // TM_E + N_EXPERTS
    p_rows = ntiles * TM_E
    tstart = jnp.arange(ntiles, dtype=jnp.int32) * TM_E
    n_used = gend[-1] // TM_E
    blk = jnp.minimum(jnp.arange(ntiles, dtype=jnp.int32), n_used - 1)
    te = jnp.sum((gend[None, :] <= (blk * TM_E)[:, None]).astype(jnp.int32), axis=1)
    te = jnp.minimum(te, N_EXPERTS - 1)
    valid = jnp.clip(gstart[te] + counts[te] - tstart, 0, TM_E)
    valid = jnp.where(tstart < gend[-1], valid, 0).astype(jnp.int32)

    x_sorted = _dispatch(hn, pos, p_rows)
    y_sorted = _experts(x_sorted, te, blk.astype(jnp.int32), valid, exp_w1,
                        exp_b1.reshape(N_EXPERTS, 1, 2 * D_FF), exp_w2,
                        exp_b2.reshape(N_EXPERTS, 1, D_MODEL))
    y_p, y_s = _combine(h, rwt, pos, y_sorted, final_norm_w.reshape(1, -1), tp, ts)
    return y_p.reshape(bp, lp, D_MODEL), y_s.reshape(bs, ls, D_MODEL)


def kernel(x_prompt, x_sample, norm1_w, w_in, gla_w2_f, gla_b_f, gla_w2_b, gla_b_b, gla_norm_w, sg_norm_w, sg_norm_b, sg_w, sg_b, w_out, norm2_w, router_w, router_b, exp_w1, exp_b1, exp_w2, exp_b2, final_norm_w):
    return _forward(x_prompt, x_sample, norm1_w, w_in, gla_w2_f, gla_b_f, gla_w2_b, gla_b_b,
                    gla_norm_w, sg_norm_w, sg_norm_b, sg_w, sg_b, w_out, norm2_w, router_w,
                    router_b, exp_w1, exp_b1, exp_w2, exp_b2, final_norm_w)
```

```python
import functools

import jax
import jax.numpy as jnp
from jax import lax
from jax.experimental import pallas as pl
from jax.experimental.pallas import tpu as pltpu

D_MODEL = 1024
GLA_HEADS = 4
GLA_DK = 64
GLA_DV = 128
GATE_RANK = 16
GATE_NORM = 16.0
GLA_CHUNK = 64
SG_GROUPS = 4
SG_CH = 128
SG_CHUNK = 128
N_EXPERTS = 32
TOP_K = 4
D_FF = 1024
SWIGLU_LIMIT = 7.0
SWIGLU_ALPHA = 1.702
EPS = 1e-5

LANES = 128
VMEM_LIMIT = 56 * 1024 * 1024

HEAD_W = GLA_HEADS * LANES
ZC_Q, ZC_K, ZC_V, ZC_R, ZC_U, ZC_VS = 0, 1, 2, 3, 4, 5
ZC_GATE_BLK = 6 * HEAD_W // LANES
ZC = 6 * HEAD_W + LANES

TM = 512
GLA_G = 256
TB = 256
TM_E = 512

NEG = -1e30


def _cparams(sem):
    return pltpu.CompilerParams(dimension_semantics=sem, vmem_limit_bytes=VMEM_LIMIT)


def _inproj_kernel(np_tiles, xp_ref, xs_ref, n1_ref, w_ref, z_ref):
    i = pl.program_id(0)
    x = jnp.where(i < np_tiles, xp_ref[...], xs_ref[...])
    ms = jnp.mean(x * x, axis=-1, keepdims=True)
    xn = (x * lax.rsqrt(ms + EPS)) * n1_ref[...]
    z = jnp.dot(xn.astype(jnp.bfloat16), w_ref[...], preferred_element_type=jnp.float32)
    z_ref[...] = z.astype(jnp.bfloat16)


def _inproj(xp, xs, n1, w_all):
    tp, ts = xp.shape[0], xs.shape[0]
    npt, nst = tp // TM, ts // TM
    return pl.pallas_call(
        functools.partial(_inproj_kernel, npt),
        grid=(npt + nst,),
        in_specs=[
            pl.BlockSpec((TM, D_MODEL), lambda i: (jnp.minimum(i, npt - 1), 0)),
            pl.BlockSpec((TM, D_MODEL), lambda i: (jnp.maximum(i - npt, 0), 0)),
            pl.BlockSpec((1, D_MODEL), lambda i: (0, 0)),
            pl.BlockSpec((D_MODEL, ZC), lambda i: (0, 0)),
        ],
        out_specs=pl.BlockSpec((TM, ZC), lambda i: (i, 0)),
        out_shape=jax.ShapeDtypeStruct((tp + ts, ZC), jnp.bfloat16),
        compiler_params=_cparams(("arbitrary",)),
        name="inproj",
    )(xp, xs, n1, w_all)


def _log_sigmoid(x):
    return jnp.minimum(x, 0.0) - jnp.log1p(jnp.exp(-jnp.abs(x)))


_NT = (((1,), (1,)), ((), ()))
_TN = (((0,), (0,)), ((), ()))


def _log_decay(gz_ref, w2_ref, gb_ref, tri_ref):
    gpre = jnp.dot(gz_ref[...], w2_ref[...], preferred_element_type=jnp.float32) + gb_ref[...]
    g = _log_sigmoid(gpre) * (1.0 / GATE_NORM)
    g_hi = g.astype(jnp.bfloat16)
    r1 = g - g_hi.astype(jnp.float32)
    g_mid = r1.astype(jnp.bfloat16)
    g_lo = (r1 - g_mid.astype(jnp.float32)).astype(jnp.bfloat16)
    tri = tri_ref[...]
    return (jnp.dot(tri, g_hi, preferred_element_type=jnp.float32)
            + jnp.dot(tri, g_mid, preferred_element_type=jnp.float32)
            + jnp.dot(tri, g_lo, preferred_element_type=jnp.float32))


def _gla_direction(z_ref, b, mask, st_ref, o_ref, reverse):
    nchunk = GLA_G // GLA_CHUNK
    order = range(nchunk - 1, -1, -1) if reverse else range(nchunk)
    for c in order:
        r0 = c * GLA_CHUNK
        last = r0 if reverse else r0 + GLA_CHUNK - 1
        rows = slice(r0, r0 + GLA_CHUNK)
        for h in range(GLA_HEADS):
            l0 = h * LANES
            bh = b[rows, l0:l0 + LANES]
            bl = b[last:last + 1, l0:l0 + LANES]
            qh = z_ref[rows, ZC_Q * HEAD_W + l0:ZC_Q * HEAD_W + l0 + LANES].astype(jnp.float32)
            kh = z_ref[rows, ZC_K * HEAD_W + l0:ZC_K * HEAD_W + l0 + LANES].astype(jnp.float32)
            vh = z_ref[rows, ZC_V * HEAD_W + l0:ZC_V * HEAD_W + l0 + LANES]
            qt = (qh * jnp.exp(bh)).astype(jnp.bfloat16)
            kt = (kh * jnp.exp(-bh)).astype(jnp.bfloat16)
            kd = (kh * jnp.exp(bl - bh)).astype(jnp.bfloat16)
            a = lax.dot_general(qt, kt, _NT, preferred_element_type=jnp.float32) * mask
            st = st_ref[h]
            o = (jnp.dot(a.astype(jnp.bfloat16), vh, preferred_element_type=jnp.float32)
                 + lax.dot_general(qt, st.astype(jnp.bfloat16), _NT,
                                   preferred_element_type=jnp.float32))
            o_ref[rows, l0:l0 + LANES] = o
            st_ref[h] = st * jnp.exp(bl) + lax.dot_general(
                vh, kd, _TN, preferred_element_type=jnp.float32)


def _gla_kernel(start_ref, n_ref, zf, gzf, zb, gzb,
                w2f, gbf, w2b, gbb, tril, triu, maskf, maskb,
                of_ref, ob_ref, stf, stb):
    s = pl.program_id(0)
    i = pl.program_id(1)

    @pl.when(i == 0)
    def _():
        stf[...] = jnp.zeros_like(stf)
        stb[...] = jnp.zeros_like(stb)

    @pl.when(i < n_ref[s])
    def _():
        b_f = _log_decay(gzf, w2f, gbf, tril)
        b_b = _log_decay(gzb, w2b, gbb, triu)
        _gla_direction(zf, b_f, maskf[...], stf, of_ref, False)
        _gla_direction(zb, b_b, maskb[...], stb, ob_ref, True)


def _gla(z, seq_start, seq_n, max_groups, w2f, gbf, w2b, gbb):
    t = z.shape[0]
    nseq = seq_start.shape[0]

    def fwd(s, i, st, n):
        return st[s] + jnp.minimum(i, n[s] - 1)

    def bwd(s, i, st, n):
        return st[s] + jnp.maximum(n[s] - 1 - i, 0)

    def zspec(width, col, blk):
        return pl.BlockSpec((GLA_G, width), lambda s, i, st, n: (blk(s, i, st, n), col))

    def const(shape):
        return pl.BlockSpec(shape, lambda s, i, st, n: (0,) * len(shape))

    r = lax.broadcasted_iota(jnp.int32, (GLA_G, GLA_G), 0)
    c = lax.broadcasted_iota(jnp.int32, (GLA_G, GLA_G), 1)
    same = (r // GLA_CHUNK) == (c // GLA_CHUNK)
    tril = (same & (c <= r)).astype(jnp.bfloat16)
    triu = (same & (c >= r)).astype(jnp.bfloat16)
    rr = lax.broadcasted_iota(jnp.int32, (GLA_CHUNK, GLA_CHUNK), 0)
    cc = lax.broadcasted_iota(jnp.int32, (GLA_CHUNK, GLA_CHUNK), 1)
    maskf = (cc <= rr).astype(jnp.float32)
    maskb = (cc > rr).astype(jnp.float32)

    qkv_w = 3 * HEAD_W
    in_specs = [
        zspec(qkv_w, 0, fwd), zspec(LANES, ZC_GATE_BLK, fwd),
        zspec(qkv_w, 0, bwd), zspec(LANES, ZC_GATE_BLK, bwd),
        const((LANES, HEAD_W)), const((1, HEAD_W)), const((LANES, HEAD_W)), const((1, HEAD_W)),
        const((GLA_G, GLA_G)), const((GLA_G, GLA_G)),
        const((GLA_CHUNK, GLA_CHUNK)), const((GLA_CHUNK, GLA_CHUNK)),
    ]
    out_specs = [
        pl.BlockSpec((GLA_G, HEAD_W), lambda s, i, st, n: (fwd(s, i, st, n), 0)),
        pl.BlockSpec((GLA_G, HEAD_W), lambda s, i, st, n: (bwd(s, i, st, n), 0)),
    ]
    grid_spec = pltpu.PrefetchScalarGridSpec(
        num_scalar_prefetch=2,
        grid=(nseq, max_groups),
        in_specs=in_specs,
        out_specs=out_specs,
        scratch_shapes=[pltpu.VMEM((GLA_HEADS, GLA_DV, LANES), jnp.float32),
                        pltpu.VMEM((GLA_HEADS, GLA_DV, LANES), jnp.float32)],
    )
    return pl.pallas_call(
        _gla_kernel,
        grid_spec=grid_spec,
        out_shape=[jax.ShapeDtypeStruct((t, HEAD_W), jnp.float32),
                   jax.ShapeDtypeStruct((t, HEAD_W), jnp.float32)],
        compiler_params=_cparams(("arbitrary", "arbitrary")),
        name="gla",
    )(seq_start, seq_n, z, z, z, z, w2f, gbf, w2b, gbb, tril, triu, maskf, maskb)


def _gelu(x):
    return 0.5 * x * (1.0 + lax.erf(x * (2.0 ** -0.5)))


def _mixer_router_kernel(np_tiles, of_ref, ob_ref, r_ref, u_ref, vs_ref, xp_ref, xs_ref,
                         gnw_ref, snw_ref, snb_ref, sgw_ref, sgb_ref, wout_ref, n2_ref,
                         rw_ref, rb_ref, tri_ref,
                         h_ref, hn_ref, ri_ref, rwt_ref, cnt_ref, carry_ref):
    i = pl.program_id(0)

    @pl.when(i == 0)
    def _():
        carry_ref[...] = jnp.zeros_like(carry_ref)

    o = of_ref[...] + ob_ref[...]
    parts = []
    for hd in range(GLA_HEADS):
        oh = o[:, hd * GLA_DV:(hd + 1) * GLA_DV]
        ms = jnp.mean(oh * oh, axis=-1, keepdims=True)
        parts.append(oh * lax.rsqrt(ms + EPS))
    on = jnp.concatenate(parts, axis=1) * gnw_ref[...]
    r = r_ref[...].astype(jnp.float32)
    gla_out = on * (r / (1.0 + jnp.exp(-r)))

    u = _gelu(u_ref[...].astype(jnp.float32))
    vf = _gelu(vs_ref[...].astype(jnp.float32))
    parts = []
    for g in range(SG_GROUPS):
        vg = vf[:, g * SG_CH:(g + 1) * SG_CH]
        mu = jnp.mean(vg, axis=-1, keepdims=True)
        vc = vg - mu
        var = jnp.mean(vc * vc, axis=-1, keepdims=True)
        parts.append(vc * lax.rsqrt(var + EPS))
    vn = (jnp.concatenate(parts, axis=1) * snw_ref[...] + snb_ref[...]).astype(jnp.bfloat16)
    nch = TM // SG_CHUNK
    res = []
    for g in range(SG_GROUPS):
        vcat = jnp.concatenate(
            [vn[c * SG_CHUNK:(c + 1) * SG_CHUNK, g * SG_CH:(g + 1) * SG_CH] for c in range(nch)],
            axis=1)
        res.append(jnp.dot(sgw_ref[g], vcat, preferred_element_type=jnp.float32))
    bias = sgb_ref[...]
    rows = []
    for c in range(nch):
        rows.append(jnp.concatenate(
            [res[g][:, c * SG_CH:(c + 1) * SG_CH] for g in range(SG_GROUPS)], axis=1) + bias)
    sg_out = u * jnp.concatenate(rows, axis=0)

    mixed = jnp.concatenate([gla_out, sg_out], axis=1).astype(jnp.bfloat16)
    x = jnp.where(i < np_tiles, xp_ref[...], xs_ref[...])
    h = x + jnp.dot(mixed, wout_ref[...], preferred_element_type=jnp.float32)
    h_ref[...] = h

    ms = jnp.mean(h * h, axis=-1, keepdims=True)
    hn = (h * lax.rsqrt(ms + EPS)) * n2_ref[...]
    hn_ref[...] = hn
    hn_hi = hn.astype(jnp.bfloat16)
    hn_lo = (hn - hn_hi.astype(jnp.float32)).astype(jnp.bfloat16)
    lg2 = jnp.dot(hn_hi, rw_ref[...], preferred_element_type=jnp.float32)
    logits = (lg2[:, :LANES] + lg2[:, LANES:]
              + jnp.dot(hn_lo, rw_ref[:, :LANES], preferred_element_type=jnp.float32)
              + rb_ref[...])
    lane = lax.broadcasted_iota(jnp.int32, logits.shape, 1)
    work = jnp.where(lane < N_EXPERTS, logits, NEG)
    top_v, top_i, onehots = [], [], []
    for _ in range(TOP_K):
        m = jnp.max(work, axis=-1, keepdims=True)
        idx = jnp.min(jnp.where(work == m, lane, LANES), axis=-1, keepdims=True)
        sel = lane == idx
        top_v.append(m)
        top_i.append(idx)
        onehots.append(sel)
        work = jnp.where(sel, NEG, work)
    e = [jnp.exp(v - top_v[0]) for v in top_v]
    den = e[0] + e[1] + e[2] + e[3]
    sel_any = onehots[0] | onehots[1] | onehots[2] | onehots[3]
    msel = sel_any.astype(jnp.bfloat16)
    before = jnp.dot(tri_ref[...], msel, preferred_element_type=jnp.float32) + carry_ref[...]
    ri = jnp.zeros(logits.shape, jnp.int32)
    rwt = jnp.zeros(logits.shape, jnp.float32)
    for k in range(TOP_K):
        rank = jnp.sum(jnp.where(onehots[k], before, 0.0), axis=-1, keepdims=True)
        ri = jnp.where(lane == k, top_i[k], ri)
        ri = jnp.where(lane == TOP_K + k, rank.astype(jnp.int32), ri)
        rwt = jnp.where(lane == k, e[k] / den, rwt)
    ri_ref[...] = ri
    rwt_ref[...] = rwt
    total = carry_ref[...] + jnp.sum(sel_any.astype(jnp.float32), axis=0, keepdims=True)
    carry_ref[...] = total
    cnt_ref[...] = total


def _mixer_router(o_f, o_b, z, xp, xs, gnw, snw, snb, sgw, sgb_full, wout, n2, rw, rb):
    t = z.shape[0]
    npt = xp.shape[0] // TM
    nt = t // TM
    tri = (lax.broadcasted_iota(jnp.int32, (TM, TM), 1)
           < lax.broadcasted_iota(jnp.int32, (TM, TM), 0)).astype(jnp.bfloat16)

    def const(shape):
        return pl.BlockSpec(shape, lambda i: (0,) * len(shape))

    def tok(width, col=0):
        return pl.BlockSpec((TM, width), lambda i: (i, col))

    in_specs = [
        tok(HEAD_W), tok(HEAD_W),
        tok(HEAD_W, ZC_R), tok(HEAD_W, ZC_U), tok(HEAD_W, ZC_VS),
        pl.BlockSpec((TM, D_MODEL), lambda i: (jnp.minimum(i, npt - 1), 0)),
        pl.BlockSpec((TM, D_MODEL), lambda i: (jnp.maximum(i - npt, 0), 0)),
        const((1, HEAD_W)), const((1, HEAD_W)), const((1, HEAD_W)),
        const((SG_GROUPS, SG_CHUNK, SG_CHUNK)), const((SG_CHUNK, HEAD_W)),
        const((D_MODEL, D_MODEL)), const((1, D_MODEL)),
        const((D_MODEL, 2 * LANES)), const((1, LANES)), const((TM, TM)),
    ]
    out_specs = [
        tok(D_MODEL), tok(D_MODEL), tok(LANES), tok(LANES), const((1, LANES)),
    ]
    out_shape = [
        jax.ShapeDtypeStruct((t, D_MODEL), jnp.float32),
        jax.ShapeDtypeStruct((t, D_MODEL), jnp.float32),
        jax.ShapeDtypeStruct((t, LANES), jnp.int32),
        jax.ShapeDtypeStruct((t, LANES), jnp.float32),
        jax.ShapeDtypeStruct((1, LANES), jnp.float32),
    ]
    return pl.pallas_call(
        functools.partial(_mixer_router_kernel, npt),
        grid=(nt,),
        in_specs=in_specs,
        out_specs=out_specs,
        out_shape=out_shape,
        scratch_shapes=[pltpu.VMEM((1, LANES), jnp.float32)],
        compiler_params=_cparams(("arbitrary",)),
        name="mixer_router",
    )(o_f, o_b, z, z, z, xp, xs, gnw, snw, snb, sgw, sgb_full, wout, n2, rw, rb, tri)


def _row_copy(src_ref, src_row, dst_ref, dst_row, sem):
    return pltpu.make_async_copy(src_ref.at[pl.ds(src_row, 1)], dst_ref.at[pl.ds(dst_row, 1)], sem)


def _dispatch_kernel(pos_ref, hn_ref, xs_ref, sem):
    def issue(t, carry):
        for k in range(TOP_K):
            _row_copy(hn_ref, t, xs_ref, pos_ref[TOP_K * t + k], sem.at[0]).start(priority=k % 2)
        return carry

    lax.fori_loop(0, TB, issue, 0, unroll=8)

    for _ in range(TOP_K):
        pltpu.make_async_copy(hn_ref, xs_ref.at[pl.ds(0, TB)], sem.at[0]).wait()


def _dispatch(hn, pos, p_rows):
    t = hn.shape[0]
    return pl.pallas_call(
        _dispatch_kernel,
        grid=(t // TB,),
        in_specs=[
            pl.BlockSpec((TOP_K * TB,), lambda i: (i,), memory_space=pltpu.SMEM),
            pl.BlockSpec((TB, D_MODEL), lambda i: (i, 0)),
        ],
        out_specs=pl.BlockSpec(memory_space=pl.ANY),
        out_shape=jax.ShapeDtypeStruct((p_rows, D_MODEL), jnp.float32),
        scratch_shapes=[pltpu.SemaphoreType.DMA((1,))],
        compiler_params=_cparams(("arbitrary",)),
        name="dispatch",
    )(pos, hn)


def _expert_kernel(te_ref, blk_ref, valid_ref, x_ref, w1_ref, b1_ref, w2_ref, b2_ref,
                   y_ref, w1b, w2b):
    j = pl.program_id(0)
    prev = te_ref[jnp.maximum(j - 1, 0)]

    @pl.when((j == 0) | (te_ref[j] != prev))
    def _():
        w1b[...] = w1_ref[0].astype(jnp.bfloat16)
        w2b[...] = w2_ref[0].astype(jnp.bfloat16)

    nvalid = valid_ref[j]

    @pl.when(nvalid == 0)
    def _():
        y_ref[...] = jnp.zeros_like(y_ref)

    @pl.when(nvalid > 0)
    def _():
        row = lax.broadcasted_iota(jnp.int32, (TM_E, 1), 0)
        x = jnp.where(row < nvalid, x_ref[...], 0.0).astype(jnp.bfloat16)
        zz = jnp.dot(x, w1b[...], preferred_element_type=jnp.float32) + b1_ref[0]
        gate = jnp.minimum(zz[:, :D_FF], SWIGLU_LIMIT)
        lin = jnp.clip(zz[:, D_FF:], -SWIGLU_LIMIT, SWIGLU_LIMIT)
        a = gate * (1.0 / (1.0 + jnp.exp(-SWIGLU_ALPHA * gate))) * (lin + 1.0)
        y_ref[...] = jnp.dot(a.astype(jnp.bfloat16), w2b[...],
                             preferred_element_type=jnp.float32) + b2_ref[0]


def _experts(x_sorted, te, blk, valid, w1, b1, w2, b2):
    p_rows = x_sorted.shape[0]
    ntiles = p_rows // TM_E
    grid_spec = pltpu.PrefetchScalarGridSpec(
        num_scalar_prefetch=3,
        grid=(ntiles,),
        in_specs=[
            pl.BlockSpec((TM_E, D_MODEL), lambda j, te, blk, v: (blk[j], 0)),
            pl.BlockSpec((1, D_MODEL, 2 * D_FF), lambda j, te, blk, v: (te[j], 0, 0)),
            pl.BlockSpec((1, 1, 2 * D_FF), lambda j, te, blk, v: (te[j], 0, 0)),
            pl.BlockSpec((1, D_FF, D_MODEL), lambda j, te, blk, v: (te[j], 0, 0)),
            pl.BlockSpec((1, 1, D_MODEL), lambda j, te, blk, v: (te[j], 0, 0)),
        ],
        out_specs=pl.BlockSpec((TM_E, D_MODEL), lambda j, te, blk, v: (j, 0)),
        scratch_shapes=[pltpu.VMEM((D_MODEL, 2 * D_FF), jnp.bfloat16),
                        pltpu.VMEM((D_FF, D_MODEL), jnp.bfloat16)],
    )
    return pl.pallas_call(
        _expert_kernel,
        grid_spec=grid_spec,
        out_shape=jax.ShapeDtypeStruct((p_rows, D_MODEL), jnp.float32),
        compiler_params=_cparams(("arbitrary",)),
        name="experts",
    )(te, blk, valid, x_sorted, w1, b1, w2, b2)


def _combine_kernel(np_tiles, pos_ref, h_ref, rwt_ref, fw_ref, ys_ref,
                    outp_ref, outs_ref, ybuf, sem):
    i = pl.program_id(0)

    def issue(t, carry):
        for k in range(TOP_K):
            _row_copy(ys_ref, pos_ref[TOP_K * t + k], ybuf.at[k], t, sem.at[0]).start(priority=k % 2)
        return carry

    lax.fori_loop(0, TB, issue, 0, unroll=8)

    for k in range(TOP_K):
        pltpu.make_async_copy(ys_ref.at[pl.ds(0, TB)], ybuf.at[k], sem.at[0]).wait()

    acc = h_ref[...]
    w = rwt_ref[...]
    for k in range(TOP_K):
        acc = acc + w[:, k:k + 1] * ybuf[k]
    ms = jnp.mean(acc * acc, axis=-1, keepdims=True)
    res = (acc * lax.rsqrt(ms + EPS)) * fw_ref[...]

    @pl.when(i < np_tiles)
    def _():
        outp_ref[...] = res

    @pl.when(i >= np_tiles)
    def _():
        outs_ref[...] = res


def _combine(h, rwt, pos, y_sorted, fw, tp, ts):
    t = h.shape[0]
    npt = tp // TB
    return pl.pallas_call(
        functools.partial(_combine_kernel, npt),
        grid=(t // TB,),
        in_specs=[
            pl.BlockSpec((TOP_K * TB,), lambda i: (i,), memory_space=pltpu.SMEM),
            pl.BlockSpec((TB, D_MODEL), lambda i: (i, 0)),
            pl.BlockSpec((TB, LANES), lambda i: (i, 0)),
            pl.BlockSpec((1, D_MODEL), lambda i: (0, 0)),
            pl.BlockSpec(memory_space=pl.ANY),
        ],
        out_specs=[
            pl.BlockSpec((TB, D_MODEL), lambda i: (jnp.minimum(i, npt - 1), 0)),
            pl.BlockSpec((TB, D_MODEL), lambda i: (jnp.maximum(i - npt, 0), 0)),
        ],
        out_shape=[jax.ShapeDtypeStruct((tp, D_MODEL), jnp.float32),
                   jax.ShapeDtypeStruct((ts, D_MODEL), jnp.float32)],
        scratch_shapes=[pltpu.VMEM((TOP_K, TB, D_MODEL), jnp.float32),
                        pltpu.SemaphoreType.DMA((1,))],
        compiler_params=_cparams(("arbitrary",)),
        name="combine",
    )(pos, h, rwt, fw, y_sorted)


def _pad_heads(w):
    rows = w.shape[0]
    w = w.reshape(rows, GLA_HEADS, GLA_DK)
    w = jnp.pad(w, ((0, 0), (0, 0), (0, LANES - GLA_DK)))
    return w.reshape(rows, HEAD_W)


def _forward(x_prompt, x_sample, norm1_w, w_in, gla_w2_f, gla_b_f, gla_w2_b, gla_b_b,
             gla_norm_w, sg_norm_w, sg_norm_b, sg_w, sg_b, w_out, norm2_w, router_w,
             router_b, exp_w1, exp_b1, exp_w2, exp_b2, final_norm_w):
    bp, lp, _ = x_prompt.shape
    bs, ls, _ = x_sample.shape
    tp, ts = bp * lp, bs * ls
    t = tp + ts
    assert lp % TM == 0 and ls % TM == 0 and TM % GLA_G == 0 and TM % TB == 0
    xp = x_prompt.reshape(tp, D_MODEL)
    xs = x_sample.reshape(ts, D_MODEL)

    k0 = GLA_HEADS * GLA_DK
    v0 = 2 * k0
    r0 = v0 + HEAD_W
    g0 = r0 + HEAD_W
    u0 = g0 + 2 * GATE_RANK
    wq = _pad_heads(w_in[:, :k0] * (GLA_DK ** -0.5))
    wk = _pad_heads(w_in[:, k0:v0])
    wgate = jnp.pad(w_in[:, g0:u0], ((0, 0), (0, LANES - 2 * GATE_RANK)))
    w_all = jnp.concatenate(
        [wq, wk, w_in[:, v0:r0], w_in[:, r0:g0], w_in[:, u0:], wgate], axis=1
    ).astype(jnp.bfloat16)

    z = _inproj(xp, xs, norm1_w.reshape(1, D_MODEL), w_all)

    w2f = jnp.zeros((LANES, HEAD_W), jnp.float32).at[:GATE_RANK].set(_pad_heads(gla_w2_f))
    w2b = jnp.zeros((LANES, HEAD_W), jnp.float32).at[GATE_RANK:2 * GATE_RANK].set(
        _pad_heads(gla_w2_b))
    gbf = _pad_heads(gla_b_f.reshape(1, -1))
    gbb = _pad_heads(gla_b_b.reshape(1, -1))
    gp, gs = lp // GLA_G, ls // GLA_G
    seq_start = jnp.concatenate([jnp.arange(bp, dtype=jnp.int32) * gp,
                                 bp * gp + jnp.arange(bs, dtype=jnp.int32) * gs])
    seq_n = jnp.concatenate([jnp.full((bp,), gp, jnp.int32), jnp.full((bs,), gs, jnp.int32)])
    o_f, o_b = _gla(z, seq_start, seq_n, max(gp, gs), w2f.astype(jnp.bfloat16), gbf,
                    w2b.astype(jnp.bfloat16), gbb)

    sgb_full = jnp.repeat(sg_b.T, SG_CH, axis=1)
    rw = jnp.pad(router_w, ((0, 0), (0, LANES - N_EXPERTS)))
    rw_hi = rw.astype(jnp.bfloat16)
    rw_lo = (rw - rw_hi.astype(jnp.float32)).astype(jnp.bfloat16)
    rw = jnp.concatenate([rw_hi, rw_lo], axis=1)
    rb = jnp.pad(router_b.reshape(1, -1), ((0, 0), (0, LANES - N_EXPERTS)))
    h, hn, ri, rwt, cnt = _mixer_router(
        o_f, o_b, z, xp, xs, gla_norm_w.reshape(1, -1), sg_norm_w.reshape(1, -1),
        sg_norm_b.reshape(1, -1), sg_w.astype(jnp.bfloat16), sgb_full,
        w_out.astype(jnp.bfloat16), norm2_w.reshape(1, -1), rw, rb)

    counts = cnt[0, :N_EXPERTS].astype(jnp.int32)
    padded = ((counts + TM_E - 1) // TM_E) * TM_E
    gend = jnp.cumsum(padded)
    gstart = gend - padded
    pos = (gstart[ri[:, :TOP_K]] + ri[:, TOP_K:2 * TOP_K]).reshape(-1)
    ntiles = (TOP_K * t) // TM_E + N_EXPERTS
    p_rows = ntiles * TM_E
    tstart = jnp.arange(ntiles, dtype=jnp.int32) * TM_E
    n_used = gend[-1] // TM_E
    blk = jnp.minimum(jnp.arange(ntiles, dtype=jnp.int32), n_used - 1)
    te = jnp.sum((gend[None, :] <= (blk * TM_E)[:, None]).astype(jnp.int32), axis=1)
    te = jnp.minimum(te, N_EXPERTS - 1)
    valid = jnp.clip(gstart[te] + counts[te] - tstart, 0, TM_E)
    valid = jnp.where(tstart < gend[-1], valid, 0).astype(jnp.int32)

    x_sorted = _dispatch(hn, pos, p_rows)
    y_sorted = _experts(x_sorted, te, blk.astype(jnp.int32), valid, exp_w1,
                        exp_b1.reshape(N_EXPERTS, 1, 2 * D_FF), exp_w2,
                        exp_b2.reshape(N_EXPERTS, 1, D_MODEL))
    y_p, y_s = _combine(h, rwt, pos, y_sorted, final_norm_w.reshape(1, -1), tp, ts)
    return y_p.reshape(bp, lp, D_MODEL), y_s.reshape(bs, ls, D_MODEL)


def kernel(x_prompt, x_sample, norm1_w, w_in, gla_w2_f, gla_b_f, gla_w2_b, gla_b_b, gla_norm_w, sg_norm_w, sg_norm_b, sg_w, sg_b, w_out, norm2_w, router_w, router_b, exp_w1, exp_b1, exp_w2, exp_b2, final_norm_w):
    return _forward(x_prompt, x_sample, norm1_w, w_in, gla_w2_f, gla_b_f, gla_w2_b, gla_b_b,
                    gla_norm_w, sg_norm_w, sg_norm_b, sg_w, sg_b, w_out, norm2_w, router_w,
                    router_b, exp_w1, exp_b1, exp_w2, exp_b2, final_norm_w)
```

```python
import functools

import jax
import jax.numpy as jnp
from jax import lax
from jax.experimental import pallas as pl
from jax.experimental.pallas import tpu as pltpu

D_MODEL = 1024
GLA_HEADS = 4
GLA_DK = 64
GLA_DV = 128
GATE_RANK = 16
GATE_NORM = 16.0
GLA_CHUNK = 64
SG_GROUPS = 4
SG_CH = 128
SG_CHUNK = 128
N_EXPERTS = 32
TOP_K = 4
D_FF = 1024
SWIGLU_LIMIT = 7.0
SWIGLU_ALPHA = 1.702
EPS = 1e-5

LANES = 128
VMEM_LIMIT = 56 * 1024 * 1024

HEAD_W = GLA_HEADS * LANES
ZC_Q, ZC_K, ZC_V, ZC_R, ZC_U, ZC_VS = 0, 1, 2, 3, 4, 5
ZC_GATE_BLK = 6 * HEAD_W // LANES
ZC = 6 * HEAD_W + LANES

TM = 512
GLA_G = 256
TM_E = 512
CH = 16
CPT = TM_E // CH
R_LOC = ((TOP_K * TM + N_EXPERTS * (CH - 1) + TM - 1) // TM) * TM
CPL = R_LOC // CH

NEG = -1e30


def _cparams(sem):
    return pltpu.CompilerParams(dimension_semantics=sem, vmem_limit_bytes=VMEM_LIMIT)


def _inproj_kernel(np_tiles, xp_ref, xs_ref, n1_ref, w_ref, z_ref):
    i = pl.program_id(0)
    x = jnp.where(i < np_tiles, xp_ref[...], xs_ref[...])
    ms = jnp.mean(x * x, axis=-1, keepdims=True)
    xn = (x * lax.rsqrt(ms + EPS)) * n1_ref[...]
    z = jnp.dot(xn.astype(jnp.bfloat16), w_ref[...], preferred_element_type=jnp.float32)
    z_ref[...] = z.astype(jnp.bfloat16)


def _inproj(xp, xs, n1, w_all):
    tp, ts = xp.shape[0], xs.shape[0]
    npt, nst = tp // TM, ts // TM
    return pl.pallas_call(
        functools.partial(_inproj_kernel, npt),
        grid=(npt + nst,),
        in_specs=[
            pl.BlockSpec((TM, D_MODEL), lambda i: (jnp.minimum(i, npt - 1), 0)),
            pl.BlockSpec((TM, D_MODEL), lambda i: (jnp.maximum(i - npt, 0), 0)),
            pl.BlockSpec((1, D_MODEL), lambda i: (0, 0)),
            pl.BlockSpec((D_MODEL, ZC), lambda i: (0, 0)),
        ],
        out_specs=pl.BlockSpec((TM, ZC), lambda i: (i, 0)),
        out_shape=jax.ShapeDtypeStruct((tp + ts, ZC), jnp.bfloat16),
        compiler_params=_cparams(("arbitrary",)),
        name="inproj",
    )(xp, xs, n1, w_all)


def _log_sigmoid(x):
    return jnp.minimum(x, 0.0) - jnp.log1p(jnp.exp(-jnp.abs(x)))


_NT = (((1,), (1,)), ((), ()))
_TN = (((0,), (0,)), ((), ()))


def _log_decay(gz_ref, w2_ref, gb_ref, tri_ref):
    gpre = jnp.dot(gz_ref[...], w2_ref[...], preferred_element_type=jnp.float32) + gb_ref[...]
    g = _log_sigmoid(gpre) * (1.0 / GATE_NORM)
    g_hi = g.astype(jnp.bfloat16)
    r1 = g - g_hi.astype(jnp.float32)
    g_mid = r1.astype(jnp.bfloat16)
    g_lo = (r1 - g_mid.astype(jnp.float32)).astype(jnp.bfloat16)
    tri = tri_ref[...]
    return (jnp.dot(tri, g_hi, preferred_element_type=jnp.float32)
            + jnp.dot(tri, g_mid, preferred_element_type=jnp.float32)
            + jnp.dot(tri, g_lo, preferred_element_type=jnp.float32))


def _gla_direction(z_ref, b, mask, st_ref, o_ref, reverse):
    nchunk = GLA_G // GLA_CHUNK
    order = range(nchunk - 1, -1, -1) if reverse else range(nchunk)
    for c in order:
        r0 = c * GLA_CHUNK
        last = r0 if reverse else r0 + GLA_CHUNK - 1
        rows = slice(r0, r0 + GLA_CHUNK)
        for h in range(GLA_HEADS):
            l0 = h * LANES
            bh = b[rows, l0:l0 + LANES]
            bl = b[last:last + 1, l0:l0 + LANES]
            qh = z_ref[rows, ZC_Q * HEAD_W + l0:ZC_Q * HEAD_W + l0 + LANES].astype(jnp.float32)
            kh = z_ref[rows, ZC_K * HEAD_W + l0:ZC_K * HEAD_W + l0 + LANES].astype(jnp.float32)
            vh = z_ref[rows, ZC_V * HEAD_W + l0:ZC_V * HEAD_W + l0 + LANES]
            qt = (qh * jnp.exp(bh)).astype(jnp.bfloat16)
            kt = (kh * jnp.exp(-bh)).astype(jnp.bfloat16)
            kd = (kh * jnp.exp(bl - bh)).astype(jnp.bfloat16)
            a = lax.dot_general(qt, kt, _NT, preferred_element_type=jnp.float32) * mask
            st = st_ref[h]
            o = (jnp.dot(a.astype(jnp.bfloat16), vh, preferred_element_type=jnp.float32)
                 + lax.dot_general(qt, st.astype(jnp.bfloat16), _NT,
                                   preferred_element_type=jnp.float32))
            o_ref[rows, l0:l0 + LANES] = o
            st_ref[h] = st * jnp.exp(bl) + lax.dot_general(
                vh, kd, _TN, preferred_element_type=jnp.float32)


def _gla_kernel(start_ref, n_ref, zf, gzf, zb, gzb,
                w2f, gbf, w2b, gbb, tril, triu, maskf, maskb,
                of_ref, ob_ref, stf, stb):
    s = pl.program_id(0)
    i = pl.program_id(1)

    @pl.when(i == 0)
    def _():
        stf[...] = jnp.zeros_like(stf)
        stb[...] = jnp.zeros_like(stb)

    @pl.when(i < n_ref[s])
    def _():
        b_f = _log_decay(gzf, w2f, gbf, tril)
        b_b = _log_decay(gzb, w2b, gbb, triu)
        _gla_direction(zf, b_f, maskf[...], stf, of_ref, False)
        _gla_direction(zb, b_b, maskb[...], stb, ob_ref, True)


def _gla(z, seq_start, seq_n, max_groups, w2f, gbf, w2b, gbb):
    t = z.shape[0]
    nseq = seq_start.shape[0]

    def fwd(s, i, st, n):
        return st[s] + jnp.minimum(i, n[s] - 1)

    def bwd(s, i, st, n):
        return st[s] + jnp.maximum(n[s] - 1 - i, 0)

    def zspec(width, col, blk):
        return pl.BlockSpec((GLA_G, width), lambda s, i, st, n: (blk(s, i, st, n), col))

    def const(shape):
        return pl.BlockSpec(shape, lambda s, i, st, n: (0,) * len(shape))

    r = lax.broadcasted_iota(jnp.int32, (GLA_G, GLA_G), 0)
    c = lax.broadcasted_iota(jnp.int32, (GLA_G, GLA_G), 1)
    same = (r // GLA_CHUNK) == (c // GLA_CHUNK)
    tril = (same & (c <= r)).astype(jnp.bfloat16)
    triu = (same & (c >= r)).astype(jnp.bfloat16)
    rr = lax.broadcasted_iota(jnp.int32, (GLA_CHUNK, GLA_CHUNK), 0)
    cc = lax.broadcasted_iota(jnp.int32, (GLA_CHUNK, GLA_CHUNK), 1)
    maskf = (cc <= rr).astype(jnp.float32)
    maskb = (cc > rr).astype(jnp.float32)

    qkv_w = 3 * HEAD_W
    in_specs = [
        zspec(qkv_w, 0, fwd), zspec(LANES, ZC_GATE_BLK, fwd),
        zspec(qkv_w, 0, bwd), zspec(LANES, ZC_GATE_BLK, bwd),
        const((LANES, HEAD_W)), const((1, HEAD_W)), const((LANES, HEAD_W)), const((1, HEAD_W)),
        const((GLA_G, GLA_G)), const((GLA_G, GLA_G)),
        const((GLA_CHUNK, GLA_CHUNK)), const((GLA_CHUNK, GLA_CHUNK)),
    ]
    out_specs = [
        pl.BlockSpec((GLA_G, HEAD_W), lambda s, i, st, n: (fwd(s, i, st, n), 0)),
        pl.BlockSpec((GLA_G, HEAD_W), lambda s, i, st, n: (bwd(s, i, st, n), 0)),
    ]
    grid_spec = pltpu.PrefetchScalarGridSpec(
        num_scalar_prefetch=2,
        grid=(nseq, max_groups),
        in_specs=in_specs,
        out_specs=out_specs,
        scratch_shapes=[pltpu.VMEM((GLA_HEADS, GLA_DV, LANES), jnp.float32),
                        pltpu.VMEM((GLA_HEADS, GLA_DV, LANES), jnp.float32)],
    )
    return pl.pallas_call(
        _gla_kernel,
        grid_spec=grid_spec,
        out_shape=[jax.ShapeDtypeStruct((t, HEAD_W), jnp.float32),
                   jax.ShapeDtypeStruct((t, HEAD_W), jnp.float32)],
        compiler_params=_cparams(("arbitrary", "arbitrary")),
        name="gla",
    )(seq_start, seq_n, z, z, z, z, w2f, gbf, w2b, gbb, tril, triu, maskf, maskb)


def _gelu(x):
    return 0.5 * x * (1.0 + lax.erf(x * (2.0 ** -0.5)))


def _mixer_router_kernel(np_tiles, of_ref, ob_ref, r_ref, u_ref, vs_ref, xp_ref, xs_ref,
                         gnw_ref, snw_ref, snb_ref, sgw_ref, sgb_ref, wout_ref, n2_ref,
                         rw_ref, rb_ref, tri_ref, upper_ref,
                         h_ref, xloc_ref, route_ref, ttab_ref):
    i = pl.program_id(0)

    o = of_ref[...] + ob_ref[...]
    parts = []
    for hd in range(GLA_HEADS):
        oh = o[:, hd * GLA_DV:(hd + 1) * GLA_DV]
        ms = jnp.mean(oh * oh, axis=-1, keepdims=True)
        parts.append(oh * lax.rsqrt(ms + EPS))
    on = jnp.concatenate(parts, axis=1) * gnw_ref[...]
    r = r_ref[...].astype(jnp.float32)
    gla_out = on * (r / (1.0 + jnp.exp(-r)))

    u = _gelu(u_ref[...].astype(jnp.float32))
    vf = _gelu(vs_ref[...].astype(jnp.float32))
    parts = []
    for g in range(SG_GROUPS):
        vg = vf[:, g * SG_CH:(g + 1) * SG_CH]
        mu = jnp.mean(vg, axis=-1, keepdims=True)
        vc = vg - mu
        var = jnp.mean(vc * vc, axis=-1, keepdims=True)
        parts.append(vc * lax.rsqrt(var + EPS))
    vn = (jnp.concatenate(parts, axis=1) * snw_ref[...] + snb_ref[...]).astype(jnp.bfloat16)
    nch = TM // SG_CHUNK
    res = []
    for g in range(SG_GROUPS):
        vcat = jnp.concatenate(
            [vn[c * SG_CHUNK:(c + 1) * SG_CHUNK, g * SG_CH:(g + 1) * SG_CH] for c in range(nch)],
            axis=1)
        res.append(jnp.dot(sgw_ref[g], vcat, preferred_element_type=jnp.float32))
    bias = sgb_ref[...]
    rows = []
    for c in range(nch):
        rows.append(jnp.concatenate(
            [res[g][:, c * SG_CH:(c + 1) * SG_CH] for g in range(SG_GROUPS)], axis=1) + bias)
    sg_out = u * jnp.concatenate(rows, axis=0)

    mixed = jnp.concatenate([gla_out, sg_out], axis=1).astype(jnp.bfloat16)
    x = jnp.where(i < np_tiles, xp_ref[...], xs_ref[...])
    h = x + jnp.dot(mixed, wout_ref[...], preferred_element_type=jnp.float32)
    h_ref[...] = h

    ms = jnp.mean(h * h, axis=-1, keepdims=True)
    hn = (h * lax.rsqrt(ms + EPS)) * n2_ref[...]
    hn_hi = hn.astype(jnp.bfloat16)
    hn_lo = (hn - hn_hi.astype(jnp.float32)).astype(jnp.bfloat16)
    lg2 = jnp.dot(hn_hi, rw_ref[...], preferred_element_type=jnp.float32)
    logits = (lg2[:, :LANES] + lg2[:, LANES:]
              + jnp.dot(hn_lo, rw_ref[:, :LANES], preferred_element_type=jnp.float32)
              + rb_ref[...])
    lane = lax.broadcasted_iota(jnp.int32, logits.shape, 1)
    work = jnp.where(lane < N_EXPERTS, logits, NEG)
    top_v, top_i, onehots = [], [], []
    for _ in range(TOP_K):
        m = jnp.max(work, axis=-1, keepdims=True)
        idx = jnp.min(jnp.where(work == m, lane, LANES), axis=-1, keepdims=True)
        sel = lane == idx
        top_v.append(m)
        top_i.append(idx)
        onehots.append(sel)
        work = jnp.where(sel, NEG, work)
    e = [jnp.exp(v - top_v[0]) for v in top_v]
    den = e[0] + e[1] + e[2] + e[3]
    sel_any = onehots[0] | onehots[1] | onehots[2] | onehots[3]
    msel = sel_any.astype(jnp.bfloat16)
    before = jnp.dot(tri_ref[...], msel, preferred_element_type=jnp.float32)
    cnt = jnp.sum(sel_any.astype(jnp.float32), axis=0, keepdims=True)
    nchunk = jnp.floor((cnt + (CH - 1)) * (1.0 / CH))
    nchunk8 = jnp.broadcast_to(nchunk, (8, LANES)).astype(jnp.bfloat16)
    cstart = jnp.dot(nchunk8, upper_ref[...], preferred_element_type=jnp.float32)
    base = before + cstart[0:1, :] * float(CH)
    route = jnp.zeros(logits.shape, jnp.float32)
    for k in range(TOP_K):
        lpos = jnp.sum(jnp.where(onehots[k], base, 0.0), axis=-1, keepdims=True)
        route = jnp.where(lane == k, lpos, route)
        route = jnp.where(lane == TOP_K + k, e[k] / den, route)
    route_ref[...] = route
    sub = lax.broadcasted_iota(jnp.int32, (8, LANES), 0)
    ttab = jnp.where(sub == 0, nchunk8.astype(jnp.float32), jnp.where(sub == 1, cstart, 0.0))
    ttab_ref[...] = ttab.astype(jnp.int32)
    lrow = jnp.transpose(route)
    for rb in range(R_LOC // TM):
        rid = (lax.broadcasted_iota(jnp.int32, (TM, TM), 0) + rb * TM).astype(jnp.float32)
        hit = rid == lrow[0:1, :]
        for k in range(1, TOP_K):
            hit = hit | (rid == lrow[k:k + 1, :])
        p = jnp.where(hit, 1.0, 0.0).astype(jnp.bfloat16)
        xloc_ref[rb * TM:(rb + 1) * TM, :] = jnp.dot(
            p, hn_hi, preferred_element_type=jnp.float32).astype(jnp.bfloat16)


def _mixer_router(o_f, o_b, z, xp, xs, gnw, snw, snb, sgw, sgb_full, wout, n2, rw, rb):
    t = z.shape[0]
    npt = xp.shape[0] // TM
    nt = t // TM
    tri = (lax.broadcasted_iota(jnp.int32, (TM, TM), 1)
           < lax.broadcasted_iota(jnp.int32, (TM, TM), 0)).astype(jnp.bfloat16)
    upper = (lax.broadcasted_iota(jnp.int32, (LANES, LANES), 0)
             < lax.broadcasted_iota(jnp.int32, (LANES, LANES), 1)).astype(jnp.bfloat16)

    def const(shape):
        return pl.BlockSpec(shape, lambda i: (0,) * len(shape))

    def tok(width, col=0):
        return pl.BlockSpec((TM, width), lambda i: (i, col))

    in_specs = [
        tok(HEAD_W), tok(HEAD_W),
        tok(HEAD_W, ZC_R), tok(HEAD_W, ZC_U), tok(HEAD_W, ZC_VS),
        pl.BlockSpec((TM, D_MODEL), lambda i: (jnp.minimum(i, npt - 1), 0)),
        pl.BlockSpec((TM, D_MODEL), lambda i: (jnp.maximum(i - npt, 0), 0)),
        const((1, HEAD_W)), const((1, HEAD_W)), const((1, HEAD_W)),
        const((SG_GROUPS, SG_CHUNK, SG_CHUNK)), const((SG_CHUNK, HEAD_W)),
        const((D_MODEL, D_MODEL)), const((1, D_MODEL)),
        const((D_MODEL, 2 * LANES)), const((1, LANES)), const((TM, TM)), const((LANES, LANES)),
    ]
    out_specs = [
        tok(D_MODEL),
        pl.BlockSpec((R_LOC, D_MODEL), lambda i: (i, 0)),
        tok(LANES),
        pl.BlockSpec((8, LANES), lambda i: (i, 0)),
    ]
    out_shape = [
        jax.ShapeDtypeStruct((t, D_MODEL), jnp.float32),
        jax.ShapeDtypeStruct((nt * R_LOC, D_MODEL), jnp.bfloat16),
        jax.ShapeDtypeStruct((t, LANES), jnp.float32),
        jax.ShapeDtypeStruct((nt * 8, LANES), jnp.int32),
    ]
    return pl.pallas_call(
        functools.partial(_mixer_router_kernel, npt),
        grid=(nt,),
        in_specs=in_specs,
        out_specs=out_specs,
        out_shape=out_shape,
        compiler_params=_cparams(("arbitrary",)),
        name="mixer_router",
    )(o_f, o_b, z, z, z, xp, xs, gnw, snw, snb, sgw, sgb_full, wout, n2, rw, rb, tri, upper)


def _chunk_copy(src_ref, src_chunk, dst_ref, dst_chunk, sem):
    return pltpu.make_async_copy(
        src_ref.at[pl.ds(pl.multiple_of(src_chunk * CH, CH), CH)],
        dst_ref.at[pl.ds(pl.multiple_of(dst_chunk * CH, CH), CH)], sem)


def _expert_kernel(tab_ref, te_ref, nv_ref, xloc_ref, w1_ref, b1_ref, w2_ref, b2_ref,
                   yloc_ref, xbuf, ybuf, w1b, w2b, gsem, ssem):
    j = pl.program_id(0)
    last = pl.num_programs(0) - 1
    slot = lax.rem(j, 2)
    nv = nv_ref[j]

    def gather(tile, buf_slot, start):
        def body(c, carry):
            cp = _chunk_copy(xloc_ref, tab_ref[tile * CPT + c], xbuf.at[buf_slot], c,
                             gsem.at[buf_slot])
            if start:
                cp.start()
            else:
                cp.wait()
            return carry
        lax.fori_loop(0, nv_ref[tile], body, 0)

    def scatter(tile, buf_slot, start):
        def body(c, carry):
            cp = _chunk_copy(ybuf.at[buf_slot], c, yloc_ref, tab_ref[tile * CPT + c],
                             ssem.at[buf_slot])
            if start:
                cp.start()
            else:
                cp.wait()
            return carry
        lax.fori_loop(0, nv_ref[tile], body, 0)

    @pl.when(j == 0)
    def _():
        gather(0, 0, True)

    gather(j, slot, False)

    @pl.when(j < last)
    def _():
        gather(j + 1, 1 - slot, True)

    prev = te_ref[jnp.maximum(j - 1, 0)]

    @pl.when((j == 0) | (te_ref[j] != prev))
    def _():
        w1b[...] = w1_ref[0].astype(jnp.bfloat16)
        w2b[...] = w2_ref[0].astype(jnp.bfloat16)

    @pl.when(j >= 2)
    def _():
        scatter(j - 2, slot, False)

    @pl.when(nv > 0)
    def _():
        row = lax.broadcasted_iota(jnp.int32, (TM_E, 1), 0)
        x = xbuf[slot]
        x = jnp.where(row < nv * CH, x, jnp.zeros_like(x))
        zz = jnp.dot(x, w1b[...], preferred_element_type=jnp.float32) + b1_ref[0]
        gate = jnp.minimum(zz[:, :D_FF], SWIGLU_LIMIT)
        lin = jnp.clip(zz[:, D_FF:], -SWIGLU_LIMIT, SWIGLU_LIMIT)
        a = gate * (1.0 / (1.0 + jnp.exp(-SWIGLU_ALPHA * gate))) * (lin + 1.0)
        y = jnp.dot(a.astype(jnp.bfloat16), w2b[...],
                    preferred_element_type=jnp.float32) + b2_ref[0]
        ybuf[slot] = y.astype(jnp.bfloat16)
        scatter(j, slot, True)

    @pl.when(j == last)
    def _():
        @pl.when(j >= 1)
        def _():
            scatter(j - 1, 1 - slot, False)
        scatter(j, slot, False)


def _experts(x_loc, tab, te, nv, w1, b1, w2, b2):
    ntiles = te.shape[0]
    grid_spec = pltpu.PrefetchScalarGridSpec(
        num_scalar_prefetch=3,
        grid=(ntiles,),
        in_specs=[
            pl.BlockSpec(memory_space=pl.ANY),
            pl.BlockSpec((1, D_MODEL, 2 * D_FF), lambda j, tab, te, nv: (te[j], 0, 0)),
            pl.BlockSpec((1, 1, 2 * D_FF), lambda j, tab, te, nv: (te[j], 0, 0)),
            pl.BlockSpec((1, D_FF, D_MODEL), lambda j, tab, te, nv: (te[j], 0, 0)),
            pl.BlockSpec((1, 1, D_MODEL), lambda j, tab, te, nv: (te[j], 0, 0)),
        ],
        out_specs=pl.BlockSpec(memory_space=pl.ANY),
        scratch_shapes=[pltpu.VMEM((2, TM_E, D_MODEL), jnp.bfloat16),
                        pltpu.VMEM((2, TM_E, D_MODEL), jnp.bfloat16),
                        pltpu.VMEM((D_MODEL, 2 * D_FF), jnp.bfloat16),
                        pltpu.VMEM((D_FF, D_MODEL), jnp.bfloat16),
                        pltpu.SemaphoreType.DMA((2,)),
                        pltpu.SemaphoreType.DMA((2,))],
    )
    return pl.pallas_call(
        _expert_kernel,
        grid_spec=grid_spec,
        out_shape=jax.ShapeDtypeStruct(x_loc.shape, jnp.bfloat16),
        input_output_aliases={3: 0},
        compiler_params=_cparams(("arbitrary",)),
        name="experts",
    )(tab, te, nv, x_loc, w1, b1, w2, b2)


def _combine_kernel(np_tiles, used_ref, h_ref, route_ref, fw_ref, yloc_ref,
                    outp_ref, outs_ref):
    i = pl.program_id(0)
    used = used_ref[i]
    route = route_ref[...]
    acc = h_ref[...]
    for rb in range(R_LOC // TM):
        rid = (lax.broadcasted_iota(jnp.int32, (TM, TM), 1) + rb * TM).astype(jnp.float32)
        w = jnp.zeros((TM, TM), jnp.float32)
        for k in range(TOP_K):
            w = w + jnp.where(rid == route[:, k:k + 1], route[:, TOP_K + k:TOP_K + k + 1], 0.0)
        row = lax.broadcasted_iota(jnp.int32, (TM, 1), 0) + rb * TM
        y = yloc_ref[rb * TM:(rb + 1) * TM, :]
        y = jnp.where(row < used, y, jnp.zeros_like(y))
        acc = acc + jnp.dot(w.astype(jnp.bfloat16), y, preferred_element_type=jnp.float32)
    ms = jnp.mean(acc * acc, axis=-1, keepdims=True)
    res = (acc * lax.rsqrt(ms + EPS)) * fw_ref[...]

    @pl.when(i < np_tiles)
    def _():
        outp_ref[...] = res

    @pl.when(i >= np_tiles)
    def _():
        outs_ref[...] = res


def _combine(h, route, used, y_loc, fw, tp, ts):
    t = h.shape[0]
    npt = tp // TM
    grid_spec = pltpu.PrefetchScalarGridSpec(
        num_scalar_prefetch=1,
        grid=(t // TM,),
        in_specs=[
            pl.BlockSpec((TM, D_MODEL), lambda i, u: (i, 0)),
            pl.BlockSpec((TM, LANES), lambda i, u: (i, 0)),
            pl.BlockSpec((1, D_MODEL), lambda i, u: (0, 0)),
            pl.BlockSpec((R_LOC, D_MODEL), lambda i, u: (i, 0)),
        ],
        out_specs=[
            pl.BlockSpec((TM, D_MODEL), lambda i, u: (jnp.minimum(i, npt - 1), 0)),
            pl.BlockSpec((TM, D_MODEL), lambda i, u: (jnp.maximum(i - npt, 0), 0)),
        ],
    )
    return pl.pallas_call(
        functools.partial(_combine_kernel, npt),
        grid_spec=grid_spec,
        out_shape=[jax.ShapeDtypeStruct((tp, D_MODEL), jnp.float32),
                   jax.ShapeDtypeStruct((ts, D_MODEL), jnp.float32)],
        compiler_params=_cparams(("arbitrary",)),
        name="combine",
    )(used, h, route, fw, y_loc)


def _pad_heads(w):
    rows = w.shape[0]
    w = w.reshape(rows, GLA_HEADS, GLA_DK)
    w = jnp.pad(w, ((0, 0), (0, 0), (0, LANES - GLA_DK)))
    return w.reshape(rows, HEAD_W)


def _forward(x_prompt, x_sample, norm1_w, w_in, gla_w2_f, gla_b_f, gla_w2_b, gla_b_b,
             gla_norm_w, sg_norm_w, sg_norm_b, sg_w, sg_b, w_out, norm2_w, router_w,
             router_b, exp_w1, exp_b1, exp_w2, exp_b2, final_norm_w):
    bp, lp, _ = x_prompt.shape
    bs, ls, _ = x_sample.shape
    tp, ts = bp * lp, bs * ls
    t = tp + ts
    assert lp % TM == 0 and ls % TM == 0 and TM % GLA_G == 0
    xp = x_prompt.reshape(tp, D_MODEL)
    xs = x_sample.reshape(ts, D_MODEL)

    k0 = GLA_HEADS * GLA_DK
    v0 = 2 * k0
    r0 = v0 + HEAD_W
    g0 = r0 + HEAD_W
    u0 = g0 + 2 * GATE_RANK
    wq = _pad_heads(w_in[:, :k0] * (GLA_DK ** -0.5))
    wk = _pad_heads(w_in[:, k0:v0])
    wgate = jnp.pad(w_in[:, g0:u0], ((0, 0), (0, LANES - 2 * GATE_RANK)))
    w_all = jnp.concatenate(
        [wq, wk, w_in[:, v0:r0], w_in[:, r0:g0], w_in[:, u0:], wgate], axis=1
    ).astype(jnp.bfloat16)

    z = _inproj(xp, xs, norm1_w.reshape(1, D_MODEL), w_all)

    w2f = jnp.zeros((LANES, HEAD_W), jnp.float32).at[:GATE_RANK].set(_pad_heads(gla_w2_f))
    w2b = jnp.zeros((LANES, HEAD_W), jnp.float32).at[GATE_RANK:2 * GATE_RANK].set(
        _pad_heads(gla_w2_b))
    gbf = _pad_heads(gla_b_f.reshape(1, -1))
    gbb = _pad_heads(gla_b_b.reshape(1, -1))
    gp, gs = lp // GLA_G, ls // GLA_G
    seq_start = jnp.concatenate([jnp.arange(bp, dtype=jnp.int32) * gp,
                                 bp * gp + jnp.arange(bs, dtype=jnp.int32) * gs])
    seq_n = jnp.concatenate([jnp.full((bp,), gp, jnp.int32), jnp.full((bs,), gs, jnp.int32)])
    o_f, o_b = _gla(z, seq_start, seq_n, max(gp, gs), w2f.astype(jnp.bfloat16), gbf,
                    w2b.astype(jnp.bfloat16), gbb)

    sgb_full = jnp.repeat(sg_b.T, SG_CH, axis=1)
    rw = jnp.pad(router_w, ((0, 0), (0, LANES - N_EXPERTS)))
    rw_hi = rw.astype(jnp.bfloat16)
    rw_lo = (rw - rw_hi.astype(jnp.float32)).astype(jnp.bfloat16)
    rw = jnp.concatenate([rw_hi, rw_lo], axis=1)
    rb = jnp.pad(router_b.reshape(1, -1), ((0, 0), (0, LANES - N_EXPERTS)))
    h, x_loc, route, ttab = _mixer_router(
        o_f, o_b, z, xp, xs, gla_norm_w.reshape(1, -1), sg_norm_w.reshape(1, -1),
        sg_norm_b.reshape(1, -1), sg_w.astype(jnp.bfloat16), sgb_full,
        w_out.astype(jnp.bfloat16), norm2_w.reshape(1, -1), rw, rb)

    nt = t // TM
    ttab = ttab.reshape(nt, 8, LANES)
    nch = ttab[:, 0, :N_EXPERTS]
    cst = ttab[:, 1, :N_EXPERTS]
    used = (jnp.sum(nch, axis=1) * CH).astype(jnp.int32)
    tot = jnp.sum(nch, axis=0)
    padded = ((tot + CPT - 1) // CPT) * CPT
    gend = jnp.cumsum(padded)
    gstart = gend - padded
    seg_len = jnp.concatenate([nch.T, (padded - tot)[:, None]], axis=1).reshape(-1)
    seg_dst = jnp.cumsum(seg_len) - seg_len
    src = jnp.arange(nt, dtype=jnp.int32)[:, None] * CPL + cst
    seg_src = jnp.concatenate([src.T, jnp.zeros((N_EXPERTS, 1), jnp.int32)], axis=1).reshape(-1)
    ntiles = (TOP_K * t // CH + nt * N_EXPERTS + CPT - 1) // CPT + N_EXPERTS
    d = jnp.arange(ntiles * CPT, dtype=jnp.int32)
    seg_of_d = jnp.sum((seg_dst[None, :] <= d[:, None]).astype(jnp.int32), axis=1) - 1
    tab = (seg_src[seg_of_d] + d - seg_dst[seg_of_d]).astype(jnp.int32)
    cstart_tile = jnp.arange(ntiles, dtype=jnp.int32) * CPT
    te = jnp.sum((gend[None, :] <= cstart_tile[:, None]).astype(jnp.int32), axis=1)
    te = jnp.minimum(te, N_EXPERTS - 1)
    nv = jnp.clip(gstart[te] + tot[te] - cstart_tile, 0, CPT)
    nv = jnp.where(cstart_tile < gend[-1], nv, 0).astype(jnp.int32)
    te = jnp.where(cstart_tile < gend[-1], te, te[jnp.maximum(gend[-1] // CPT - 1, 0)])
    te = te.astype(jnp.int32)

    y_loc = _experts(x_loc, tab, te, nv, exp_w1, exp_b1.reshape(N_EXPERTS, 1, 2 * D_FF),
                     exp_w2, exp_b2.reshape(N_EXPERTS, 1, D_MODEL))
    y_p, y_s = _combine(h, route, used, y_loc, final_norm_w.reshape(1, -1), tp, ts)
    return y_p.reshape(bp, lp, D_MODEL), y_s.reshape(bs, ls, D_MODEL)


def kernel(x_prompt, x_sample, norm1_w, w_in, gla_w2_f, gla_b_f, gla_w2_b, gla_b_b, gla_norm_w, sg_norm_w, sg_norm_b, sg_w, sg_b, w_out, norm2_w, router_w, router_b, exp_w1, exp_b1, exp_w2, exp_b2, final_norm_w):
    return _forward(x_prompt, x_sample, norm1_w, w_in, gla_w2_f, gla_b_f, gla_w2_b, gla_b_b,
                    gla_norm_w, sg_norm_w, sg_norm_b, sg_w, sg_b, w_out, norm2_w, router_w,
                    router_b, exp_w1, exp_b1, exp_w2, exp_b2, final_norm_w)
```

```python
import functools

import jax
import jax.numpy as jnp
from jax import lax
from jax.experimental import pallas as pl
from jax.experimental.pallas import tpu as pltpu

D_MODEL = 1024
GLA_HEADS = 4
GLA_DK = 64
GLA_DV = 128
GATE_RANK = 16
GATE_NORM = 16.0
GLA_CHUNK = 64
SG_GROUPS = 4
SG_CH = 128
SG_CHUNK = 128
N_EXPERTS = 32
TOP_K = 4
D_FF = 1024
SWIGLU_LIMIT = 7.0
SWIGLU_ALPHA = 1.702
EPS = 1e-5

LANES = 128
VMEM_LIMIT = 56 * 1024 * 1024

HEAD_W = GLA_HEADS * GLA_DV
KEY_W = GLA_HEADS * GLA_DK
HEAD_PAIRS = KEY_W // LANES
Z_Q, Z_K, Z_V = 0, KEY_W, 2 * KEY_W
QKV_W = 2 * KEY_W + HEAD_W
ZC_R, ZC_U, ZC_VS = 2, 3, 4
ZC_GATE_BLK = (QKV_W + 3 * HEAD_W) // LANES
ZC = QKV_W + 3 * HEAD_W + LANES

TM = 512
GLA_G = 512
TRI_W = 256
TM_E = 512
CH = 16
CPT = TM_E // CH
R_LOC = ((TOP_K * TM + N_EXPERTS * (CH - 1) + TM - 1) // TM) * TM
CPL = R_LOC // CH

NEG = -1e30


def _cparams(sem):
    return pltpu.CompilerParams(dimension_semantics=sem, vmem_limit_bytes=VMEM_LIMIT)


def _inproj_kernel(np_tiles, xp_ref, xs_ref, n1_ref, w_ref, z_ref):
    i = pl.program_id(0)
    x = jnp.where(i < np_tiles, xp_ref[...], xs_ref[...])
    ms = jnp.mean(x * x, axis=-1, keepdims=True)
    xn = (x * lax.rsqrt(ms + EPS)) * n1_ref[...]
    z = jnp.dot(xn.astype(jnp.bfloat16), w_ref[...], preferred_element_type=jnp.float32)
    z_ref[...] = z.astype(jnp.bfloat16)


def _inproj(xp, xs, n1, w_all):
    tp, ts = xp.shape[0], xs.shape[0]
    npt, nst = tp // TM, ts // TM
    return pl.pallas_call(
        functools.partial(_inproj_kernel, npt),
        grid=(npt + nst,),
        in_specs=[
            pl.BlockSpec((TM, D_MODEL), lambda i: (jnp.minimum(i, npt - 1), 0)),
            pl.BlockSpec((TM, D_MODEL), lambda i: (jnp.maximum(i - npt, 0), 0)),
            pl.BlockSpec((1, D_MODEL), lambda i: (0, 0)),
            pl.BlockSpec((D_MODEL, ZC), lambda i: (0, 0)),
        ],
        out_specs=pl.BlockSpec((TM, ZC), lambda i: (i, 0)),
        out_shape=jax.ShapeDtypeStruct((tp + ts, ZC), jnp.bfloat16),
        compiler_params=_cparams(("arbitrary",)),
        name="inproj",
    )(xp, xs, n1, w_all)


def _log_sigmoid(x):
    return jnp.minimum(x, 0.0) - jnp.log1p(jnp.exp(-jnp.abs(x)))


_NT = (((1,), (1,)), ((), ()))
_TN = (((0,), (0,)), ((), ()))


def _log_decay(gz_ref, w2_ref, gb_ref, tri_ref):
    gpre = jnp.dot(gz_ref[...], w2_ref[...], preferred_element_type=jnp.float32) + gb_ref[...]
    g = _log_sigmoid(gpre) * (1.0 / GATE_NORM)
    g_hi = g.astype(jnp.bfloat16)
    r1 = g - g_hi.astype(jnp.float32)
    g_mid = r1.astype(jnp.bfloat16)
    g_lo = (r1 - g_mid.astype(jnp.float32)).astype(jnp.bfloat16)
    tri = tri_ref[...]
    parts = []
    for s0 in range(0, GLA_G, TRI_W):
        blk = slice(s0, s0 + TRI_W)
        parts.append(jnp.dot(tri, g_hi[blk], preferred_element_type=jnp.float32)
                     + jnp.dot(tri, g_mid[blk], preferred_element_type=jnp.float32)
                     + jnp.dot(tri, g_lo[blk], preferred_element_type=jnp.float32))
    return jnp.concatenate(parts, axis=0)


def _gla_direction(z_ref, b, mask2, st_ref, o_ref, reverse):
    nchunk = GLA_G // GLA_CHUNK
    order = range(nchunk - 1, -1, -1) if reverse else range(nchunk)
    first = lax.broadcasted_iota(jnp.int32, (1, LANES), 1) < GLA_DK
    for c in order:
        r0 = c * GLA_CHUNK
        last = r0 if reverse else r0 + GLA_CHUNK - 1
        rows = slice(r0, r0 + GLA_CHUNK)
        for p in range(HEAD_PAIRS):
            l0 = p * LANES
            bp = b[rows, l0:l0 + LANES]
            bl = b[last:last + 1, l0:l0 + LANES]
            qp = z_ref[rows, Z_Q + l0:Z_Q + l0 + LANES].astype(jnp.float32)
            kp = z_ref[rows, Z_K + l0:Z_K + l0 + LANES].astype(jnp.float32)
            v0 = Z_V + 2 * p * GLA_DV
            vpair = z_ref[rows, v0:v0 + 2 * GLA_DV]
            qt = qp * jnp.exp(bp)
            kt = (kp * jnp.exp(-bp)).astype(jnp.bfloat16)
            kd = (kp * jnp.exp(bl - bp)).astype(jnp.bfloat16)
            q2 = jnp.concatenate([jnp.where(first, qt, 0.0), jnp.where(first, 0.0, qt)],
                                 axis=0).astype(jnp.bfloat16)
            a2 = (lax.dot_general(q2, kt, _NT, preferred_element_type=jnp.float32)
                  * mask2).astype(jnp.bfloat16)
            st = st_ref[p]
            inter = lax.dot_general(q2, st.astype(jnp.bfloat16), _NT,
                                    preferred_element_type=jnp.float32)
            for hh in range(2):
                hr = slice(hh * GLA_CHUNK, (hh + 1) * GLA_CHUNK)
                vh = vpair[:, hh * GLA_DV:(hh + 1) * GLA_DV]
                o = jnp.dot(a2[hr], vh, preferred_element_type=jnp.float32) + inter[hr]
                oc = (2 * p + hh) * GLA_DV
                o_ref[rows, oc:oc + GLA_DV] = o
            u2 = lax.dot_general(vpair, kd, _TN, preferred_element_type=jnp.float32)
            st_ref[p] = st * jnp.exp(bl) + jnp.where(first, u2[:GLA_DV], u2[GLA_DV:])


def _gla_kernel(start_ref, n_ref, zf, gzf, zb, gzb,
                w2f, gbf, w2b, gbb, tril, triu, maskf, maskb,
                of_ref, ob_ref, stf, stb):
    s = pl.program_id(0)
    i = pl.program_id(1)

    @pl.when(i == 0)
    def _():
        stf[...] = jnp.zeros_like(stf)
        stb[...] = jnp.zeros_like(stb)

    @pl.when(i < n_ref[s])
    def _():
        b_f = _log_decay(gzf, w2f, gbf, tril)
        b_b = _log_decay(gzb, w2b, gbb, triu)
        _gla_direction(zf, b_f, maskf[...], stf, of_ref, False)
        _gla_direction(zb, b_b, maskb[...], stb, ob_ref, True)


def _gla(z, seq_start, seq_n, max_groups, w2f, gbf, w2b, gbb):
    t = z.shape[0]
    nseq = seq_start.shape[0]

    def fwd(s, i, st, n):
        return st[s] + jnp.minimum(i, n[s] - 1)

    def bwd(s, i, st, n):
        return st[s] + jnp.maximum(n[s] - 1 - i, 0)

    def zspec(width, col, blk):
        return pl.BlockSpec((GLA_G, width), lambda s, i, st, n: (blk(s, i, st, n), col))

    def const(shape):
        return pl.BlockSpec(shape, lambda s, i, st, n: (0,) * len(shape))

    r = lax.broadcasted_iota(jnp.int32, (TRI_W, TRI_W), 0)
    c = lax.broadcasted_iota(jnp.int32, (TRI_W, TRI_W), 1)
    same = (r // GLA_CHUNK) == (c // GLA_CHUNK)
    tril = (same & (c <= r)).astype(jnp.bfloat16)
    triu = (same & (c >= r)).astype(jnp.bfloat16)
    rr = lax.broadcasted_iota(jnp.int32, (GLA_CHUNK, GLA_CHUNK), 0)
    cc = lax.broadcasted_iota(jnp.int32, (GLA_CHUNK, GLA_CHUNK), 1)
    maskf = (cc <= rr).astype(jnp.float32)
    maskb = (cc > rr).astype(jnp.float32)
    maskf = jnp.concatenate([maskf, maskf], axis=0)
    maskb = jnp.concatenate([maskb, maskb], axis=0)

    in_specs = [
        zspec(QKV_W, 0, fwd), zspec(LANES, ZC_GATE_BLK, fwd),
        zspec(QKV_W, 0, bwd), zspec(LANES, ZC_GATE_BLK, bwd),
        const((LANES, KEY_W)), const((1, KEY_W)), const((LANES, KEY_W)), const((1, KEY_W)),
        const((TRI_W, TRI_W)), const((TRI_W, TRI_W)),
        const((2 * GLA_CHUNK, GLA_CHUNK)), const((2 * GLA_CHUNK, GLA_CHUNK)),
    ]
    out_specs = [
        pl.BlockSpec((GLA_G, HEAD_W), lambda s, i, st, n: (fwd(s, i, st, n), 0)),
        pl.BlockSpec((GLA_G, HEAD_W), lambda s, i, st, n: (bwd(s, i, st, n), 0)),
    ]
    grid_spec = pltpu.PrefetchScalarGridSpec(
        num_scalar_prefetch=2,
        grid=(nseq, max_groups),
        in_specs=in_specs,
        out_specs=out_specs,
        scratch_shapes=[pltpu.VMEM((HEAD_PAIRS, GLA_DV, LANES), jnp.float32),
                        pltpu.VMEM((HEAD_PAIRS, GLA_DV, LANES), jnp.float32)],
    )
    return pl.pallas_call(
        _gla_kernel,
        grid_spec=grid_spec,
        out_shape=[jax.ShapeDtypeStruct((t, HEAD_W), jnp.float32),
                   jax.ShapeDtypeStruct((t, HEAD_W), jnp.float32)],
        compiler_params=_cparams(("arbitrary", "arbitrary")),
        name="gla",
    )(seq_start, seq_n, z, z, z, z, w2f, gbf, w2b, gbb, tril, triu, maskf, maskb)


def _gelu(x):
    return 0.5 * x * (1.0 + lax.erf(x * (2.0 ** -0.5)))


def _mixer_router_kernel(np_tiles, of_ref, ob_ref, r_ref, u_ref, vs_ref, xp_ref, xs_ref,
                         gnw_ref, snw_ref, snb_ref, sgw_ref, sgb_ref, wout_ref, n2_ref,
                         rw_ref, rb_ref, tri_ref, upper_ref,
                         h_ref, xloc_ref, route_ref, ttab_ref):
    i = pl.program_id(0)

    o = of_ref[...] + ob_ref[...]
    parts = []
    for hd in range(GLA_HEADS):
        oh = o[:, hd * GLA_DV:(hd + 1) * GLA_DV]
        ms = jnp.mean(oh * oh, axis=-1, keepdims=True)
        parts.append(oh * lax.rsqrt(ms + EPS))
    on = jnp.concatenate(parts, axis=1) * gnw_ref[...]
    r = r_ref[...].astype(jnp.float32)
    gla_out = on * (r / (1.0 + jnp.exp(-r)))

    u = _gelu(u_ref[...].astype(jnp.float32))
    vf = _gelu(vs_ref[...].astype(jnp.float32))
    parts = []
    for g in range(SG_GROUPS):
        vg = vf[:, g * SG_CH:(g + 1) * SG_CH]
        mu = jnp.mean(vg, axis=-1, keepdims=True)
        vc = vg - mu
        var = jnp.mean(vc * vc, axis=-1, keepdims=True)
        parts.append(vc * lax.rsqrt(var + EPS))
    vn = (jnp.concatenate(parts, axis=1) * snw_ref[...] + snb_ref[...]).astype(jnp.bfloat16)
    nch = TM // SG_CHUNK
    res = []
    for g in range(SG_GROUPS):
        vcat = jnp.concatenate(
            [vn[c * SG_CHUNK:(c + 1) * SG_CHUNK, g * SG_CH:(g + 1) * SG_CH] for c in range(nch)],
            axis=1)
        res.append(jnp.dot(sgw_ref[g], vcat, preferred_element_type=jnp.float32))
    bias = sgb_ref[...]
    rows = []
    for c in range(nch):
        rows.append(jnp.concatenate(
            [res[g][:, c * SG_CH:(c + 1) * SG_CH] for g in range(SG_GROUPS)], axis=1) + bias)
    sg_out = u * jnp.concatenate(rows, axis=0)

    mixed = jnp.concatenate([gla_out, sg_out], axis=1).astype(jnp.bfloat16)
    x = jnp.where(i < np_tiles, xp_ref[...], xs_ref[...])
    h = x + jnp.dot(mixed, wout_ref[...], preferred_element_type=jnp.float32)
    h_ref[...] = h

    ms = jnp.mean(h * h, axis=-1, keepdims=True)
    hn = (h * lax.rsqrt(ms + EPS)) * n2_ref[...]
    hn_hi = hn.astype(jnp.bfloat16)
    hn_lo = (hn - hn_hi.astype(jnp.float32)).astype(jnp.bfloat16)
    lg2 = jnp.dot(hn_hi, rw_ref[...], preferred_element_type=jnp.float32)
    logits = (lg2[:, :LANES] + lg2[:, LANES:]
              + jnp.dot(hn_lo, rw_ref[:, :LANES], preferred_element_type=jnp.float32)
              + rb_ref[...])
    lane = lax.broadcasted_iota(jnp.int32, logits.shape, 1)
    work = jnp.where(lane < N_EXPERTS, logits, NEG)
    top_v, top_i, onehots = [], [], []
    for _ in range(TOP_K):
        m = jnp.max(work, axis=-1, keepdims=True)
        idx = jnp.min(jnp.where(work == m, lane, LANES), axis=-1, keepdims=True)
        sel = lane == idx
        top_v.append(m)
        top_i.append(idx)
        onehots.append(sel)
        work = jnp.where(sel, NEG, work)
    e = [jnp.exp(v - top_v[0]) for v in top_v]
    den = e[0] + e[1] + e[2] + e[3]
    sel_any = onehots[0] | onehots[1] | onehots[2] | onehots[3]
    msel = sel_any.astype(jnp.bfloat16)
    before = jnp.dot(tri_ref[...], msel, preferred_element_type=jnp.float32)
    cnt = jnp.sum(sel_any.astype(jnp.float32), axis=0, keepdims=True)
    nchunk = jnp.floor((cnt + (CH - 1)) * (1.0 / CH))
    nchunk8 = jnp.broadcast_to(nchunk, (8, LANES)).astype(jnp.bfloat16)
    cstart = jnp.dot(nchunk8, upper_ref[...], preferred_element_type=jnp.float32)
    base = before + cstart[0:1, :] * float(CH)
    route = jnp.zeros(logits.shape, jnp.float32)
    for k in range(TOP_K):
        lpos = jnp.sum(jnp.where(onehots[k], base, 0.0), axis=-1, keepdims=True)
        route = jnp.where(lane == k, lpos, route)
        route = jnp.where(lane == TOP_K + k, e[k] / den, route)
    route_ref[...] = route
    sub = lax.broadcasted_iota(jnp.int32, (8, LANES), 0)
    ttab = jnp.where(sub == 0, nchunk8.astype(jnp.float32), jnp.where(sub == 1, cstart, 0.0))
    ttab_ref[...] = ttab.astype(jnp.int32)
    lrow = jnp.transpose(route)
    for rb in range(R_LOC // TM):
        rid = (lax.broadcasted_iota(jnp.int32, (TM, TM), 0) + rb * TM).astype(jnp.float32)
        hit = rid == lrow[0:1, :]
        for k in range(1, TOP_K):
            hit = hit | (rid == lrow[k:k + 1, :])
        p = jnp.where(hit, 1.0, 0.0).astype(jnp.bfloat16)
        xloc_ref[rb * TM:(rb + 1) * TM, :] = jnp.dot(
            p, hn_hi, preferred_element_type=jnp.float32).astype(jnp.bfloat16)


def _mixer_router(o_f, o_b, z, xp, xs, gnw, snw, snb, sgw, sgb_full, wout, n2, rw, rb):
    t = z.shape[0]
    npt = xp.shape[0] // TM
    nt = t // TM
    tri = (lax.broadcasted_iota(jnp.int32, (TM, TM), 1)
           < lax.broadcasted_iota(jnp.int32, (TM, TM), 0)).astype(jnp.bfloat16)
    upper = (lax.broadcasted_iota(jnp.int32, (LANES, LANES), 0)
             < lax.broadcasted_iota(jnp.int32, (LANES, LANES), 1)).astype(jnp.bfloat16)

    def const(shape):
        return pl.BlockSpec(shape, lambda i: (0,) * len(shape))

    def tok(width, col=0):
        return pl.BlockSpec((TM, width), lambda i: (i, col))

    in_specs = [
        tok(HEAD_W), tok(HEAD_W),
        tok(HEAD_W, ZC_R), tok(HEAD_W, ZC_U), tok(HEAD_W, ZC_VS),
        pl.BlockSpec((TM, D_MODEL), lambda i: (jnp.minimum(i, npt - 1), 0)),
        pl.BlockSpec((TM, D_MODEL), lambda i: (jnp.maximum(i - npt, 0), 0)),
        const((1, HEAD_W)), const((1, HEAD_W)), const((1, HEAD_W)),
        const((SG_GROUPS, SG_CHUNK, SG_CHUNK)), const((SG_CHUNK, HEAD_W)),
        const((D_MODEL, D_MODEL)), const((1, D_MODEL)),
        const((D_MODEL, 2 * LANES)), const((1, LANES)), const((TM, TM)), const((LANES, LANES)),
    ]
    out_specs = [
        tok(D_MODEL),
        pl.BlockSpec((R_LOC, D_MODEL), lambda i: (i, 0)),
        tok(LANES),
        pl.BlockSpec((8, LANES), lambda i: (i, 0)),
    ]
    out_shape = [
        jax.ShapeDtypeStruct((t, D_MODEL), jnp.float32),
        jax.ShapeDtypeStruct((nt * R_LOC, D_MODEL), jnp.bfloat16),
        jax.ShapeDtypeStruct((t, LANES), jnp.float32),
        jax.ShapeDtypeStruct((nt * 8, LANES), jnp.int32),
    ]
    return pl.pallas_call(
        functools.partial(_mixer_router_kernel, npt),
        grid=(nt,),
        in_specs=in_specs,
        out_specs=out_specs,
        out_shape=out_shape,
        compiler_params=_cparams(("arbitrary",)),
        name="mixer_router",
    )(o_f, o_b, z, z, z, xp, xs, gnw, snw, snb, sgw, sgb_full, wout, n2, rw, rb, tri, upper)


def _chunk_rows(chunk):
    start = chunk * CH
    return pl.ds(start if isinstance(chunk, int) else pl.multiple_of(start, CH), CH)


def _chunk_copy(src_ref, src_chunk, dst_ref, dst_chunk, sem):
    return pltpu.make_async_copy(src_ref.at[_chunk_rows(src_chunk)],
                                 dst_ref.at[_chunk_rows(dst_chunk)], sem)


def _expert_kernel(tab_ref, te_ref, nv_ref, xloc_ref, w1_ref, b1_ref, w2_ref, b2_ref,
                   yloc_ref, xbuf, ybuf, w1b, w2b, gsem, ssem):
    j = pl.program_id(0)
    last = pl.num_programs(0) - 1
    slot = lax.rem(j, 2)
    nv = nv_ref[j]

    def move(tile, start, chunk_copy, full_copy):
        n = nv_ref[tile]

        @pl.when(n == CPT)
        def _():
            if start:
                for c in range(CPT):
                    chunk_copy(c).start()
            else:
                full_copy().wait()

        @pl.when(n < CPT)
        def _():
            def body(c, carry):
                if start:
                    chunk_copy(c).start()
                else:
                    chunk_copy(c).wait()
                return carry
            lax.fori_loop(0, n, body, 0)

    def gather(tile, buf_slot, start):
        move(tile, start,
             lambda c: _chunk_copy(xloc_ref, tab_ref[tile * CPT + c], xbuf.at[buf_slot], c,
                                   gsem.at[buf_slot]),
             lambda: pltpu.make_async_copy(xloc_ref.at[pl.ds(0, TM_E)], xbuf.at[buf_slot],
                                           gsem.at[buf_slot]))

    def scatter(tile, buf_slot, start):
        move(tile, start,
             lambda c: _chunk_copy(ybuf.at[buf_slot], c, yloc_ref, tab_ref[tile * CPT + c],
                                   ssem.at[buf_slot]),
             lambda: pltpu.make_async_copy(ybuf.at[buf_slot], yloc_ref.at[pl.ds(0, TM_E)],
                                           ssem.at[buf_slot]))

    @pl.when(j == 0)
    def _():
        gather(0, 0, True)

    gather(j, slot, False)

    @pl.when(j < last)
    def _():
        gather(j + 1, 1 - slot, True)

    prev = te_ref[jnp.maximum(j - 1, 0)]

    @pl.when((j == 0) | (te_ref[j] != prev))
    def _():
        w1b[...] = w1_ref[0].astype(jnp.bfloat16)
        w2b[...] = w2_ref[0].astype(jnp.bfloat16)

    @pl.when(j >= 2)
    def _():
        scatter(j - 2, slot, False)

    @pl.when(nv > 0)
    def _():
        row = lax.broadcasted_iota(jnp.int32, (TM_E, 1), 0)
        x = xbuf[slot]
        x = jnp.where(row < nv * CH, x, jnp.zeros_like(x))
        zz = jnp.dot(x, w1b[...], preferred_element_type=jnp.float32) + b1_ref[0]
        gate = jnp.minimum(zz[:, :D_FF], SWIGLU_LIMIT)
        lin = jnp.clip(zz[:, D_FF:], -SWIGLU_LIMIT, SWIGLU_LIMIT)
        a = gate * (1.0 / (1.0 + jnp.exp(-SWIGLU_ALPHA * gate))) * (lin + 1.0)
        y = jnp.dot(a.astype(jnp.bfloat16), w2b[...],
                    preferred_element_type=jnp.float32) + b2_ref[0]
        ybuf[slot] = y.astype(jnp.bfloat16)
        scatter(j, slot, True)

    @pl.when(j == last)
    def _():
        @pl.when(j >= 1)
        def _():
            scatter(j - 1, 1 - slot, False)
        scatter(j, slot, False)


def _experts(x_loc, tab, te, nv, w1, b1, w2, b2):
    ntiles = te.shape[0]
    grid_spec = pltpu.PrefetchScalarGridSpec(
        num_scalar_prefetch=3,
        grid=(ntiles,),
        in_specs=[
            pl.BlockSpec(memory_space=pl.ANY),
            pl.BlockSpec((1, D_MODEL, 2 * D_FF), lambda j, tab, te, nv: (te[j], 0, 0)),
            pl.BlockSpec((1, 1, 2 * D_FF), lambda j, tab, te, nv: (te[j], 0, 0)),
            pl.BlockSpec((1, D_FF, D_MODEL), lambda j, tab, te, nv: (te[j], 0, 0)),
            pl.BlockSpec((1, 1, D_MODEL), lambda j, tab, te, nv: (te[j], 0, 0)),
        ],
        out_specs=pl.BlockSpec(memory_space=pl.ANY),
        scratch_shapes=[pltpu.VMEM((2, TM_E, D_MODEL), jnp.bfloat16),
                        pltpu.VMEM((2, TM_E, D_MODEL), jnp.bfloat16),
                        pltpu.VMEM((D_MODEL, 2 * D_FF), jnp.bfloat16),
                        pltpu.VMEM((D_FF, D_MODEL), jnp.bfloat16),
                        pltpu.SemaphoreType.DMA((2,)),
                        pltpu.SemaphoreType.DMA((2,))],
    )
    return pl.pallas_call(
        _expert_kernel,
        grid_spec=grid_spec,
        out_shape=jax.ShapeDtypeStruct(x_loc.shape, jnp.bfloat16),
        input_output_aliases={3: 0},
        compiler_params=_cparams(("arbitrary",)),
        name="experts",
    )(tab, te, nv, x_loc, w1, b1, w2, b2)


def _combine_kernel(np_tiles, used_ref, h_ref, route_ref, fw_ref, yloc_ref,
                    outp_ref, outs_ref):
    i = pl.program_id(0)
    used = used_ref[i]
    route = route_ref[...]
    acc = h_ref[...]
    for rb in range(R_LOC // TM):
        rid = (lax.broadcasted_iota(jnp.int32, (TM, TM), 1) + rb * TM).astype(jnp.float32)
        w = jnp.zeros((TM, TM), jnp.float32)
        for k in range(TOP_K):
            w = w + jnp.where(rid == route[:, k:k + 1], route[:, TOP_K + k:TOP_K + k + 1], 0.0)
        row = lax.broadcasted_iota(jnp.int32, (TM, 1), 0) + rb * TM
        y = yloc_ref[rb * TM:(rb + 1) * TM, :]
        y = jnp.where(row < used, y, jnp.zeros_like(y))
        acc = acc + jnp.dot(w.astype(jnp.bfloat16), y, preferred_element_type=jnp.float32)
    ms = jnp.mean(acc * acc, axis=-1, keepdims=True)
    res = (acc * lax.rsqrt(ms + EPS)) * fw_ref[...]

    @pl.when(i < np_tiles)
    def _():
        outp_ref[...] = res

    @pl.when(i >= np_tiles)
    def _():
        outs_ref[...] = res


def _combine(h, route, used, y_loc, fw, tp, ts):
    t = h.shape[0]
    npt = tp // TM
    grid_spec = pltpu.PrefetchScalarGridSpec(
        num_scalar_prefetch=1,
        grid=(t // TM,),
        in_specs=[
            pl.BlockSpec((TM, D_MODEL), lambda i, u: (i, 0)),
            pl.BlockSpec((TM, LANES), lambda i, u: (i, 0)),
            pl.BlockSpec((1, D_MODEL), lambda i, u: (0, 0)),
            pl.BlockSpec((R_LOC, D_MODEL), lambda i, u: (i, 0)),
        ],
        out_specs=[
            pl.BlockSpec((TM, D_MODEL), lambda i, u: (jnp.minimum(i, npt - 1), 0)),
            pl.BlockSpec((TM, D_MODEL), lambda i, u: (jnp.maximum(i - npt, 0), 0)),
        ],
    )
    return pl.pallas_call(
        functools.partial(_combine_kernel, npt),
        grid_spec=grid_spec,
        out_shape=[jax.ShapeDtypeStruct((tp, D_MODEL), jnp.float32),
                   jax.ShapeDtypeStruct((ts, D_MODEL), jnp.float32)],
        compiler_params=_cparams(("arbitrary",)),
        name="combine",
    )(used, h, route, fw, y_loc)


def _forward(x_prompt, x_sample, norm1_w, w_in, gla_w2_f, gla_b_f, gla_w2_b, gla_b_b,
             gla_norm_w, sg_norm_w, sg_norm_b, sg_w, sg_b, w_out, norm2_w, router_w,
             router_b, exp_w1, exp_b1, exp_w2, exp_b2, final_norm_w):
    bp, lp, _ = x_prompt.shape
    bs, ls, _ = x_sample.shape
    tp, ts = bp * lp, bs * ls
    t = tp + ts
    assert lp % TM == 0 and ls % TM == 0 and TM % GLA_G == 0
    xp = x_prompt.reshape(tp, D_MODEL)
    xs = x_sample.reshape(ts, D_MODEL)

    g0 = QKV_W + HEAD_W
    u0 = g0 + 2 * GATE_RANK
    wgate = jnp.pad(w_in[:, g0:u0], ((0, 0), (0, LANES - 2 * GATE_RANK)))
    w_all = jnp.concatenate(
        [w_in[:, :KEY_W] * (GLA_DK ** -0.5), w_in[:, KEY_W:g0], w_in[:, u0:], wgate], axis=1
    ).astype(jnp.bfloat16)

    z = _inproj(xp, xs, norm1_w.reshape(1, D_MODEL), w_all)

    w2f = jnp.zeros((LANES, KEY_W), jnp.float32).at[:GATE_RANK].set(gla_w2_f)
    w2b = jnp.zeros((LANES, KEY_W), jnp.float32).at[GATE_RANK:2 * GATE_RANK].set(gla_w2_b)
    gbf = gla_b_f.reshape(1, -1)
    gbb = gla_b_b.reshape(1, -1)
    gp, gs = lp // GLA_G, ls // GLA_G
    seq_start = jnp.concatenate([jnp.arange(bp, dtype=jnp.int32) * gp,
                                 bp * gp + jnp.arange(bs, dtype=jnp.int32) * gs])
    seq_n = jnp.concatenate([jnp.full((bp,), gp, jnp.int32), jnp.full((bs,), gs, jnp.int32)])
    o_f, o_b = _gla(z, seq_start, seq_n, max(gp, gs), w2f.astype(jnp.bfloat16), gbf,
                    w2b.astype(jnp.bfloat16), gbb)

    sgb_full = jnp.repeat(sg_b.T, SG_CH, axis=1)
    rw = jnp.pad(router_w, ((0, 0), (0, LANES - N_EXPERTS)))
    rw_hi = rw.astype(jnp.bfloat16)
    rw_lo = (rw - rw_hi.astype(jnp.float32)).astype(jnp.bfloat16)
    rw = jnp.concatenate([rw_hi, rw_lo], axis=1)
    rb = jnp.pad(router_b.reshape(1, -1), ((0, 0), (0, LANES - N_EXPERTS)))
    h, x_loc, route, ttab = _mixer_router(
        o_f, o_b, z, xp, xs, gla_norm_w.reshape(1, -1), sg_norm_w.reshape(1, -1),
        sg_norm_b.reshape(1, -1), sg_w.astype(jnp.bfloat16), sgb_full,
        w_out.astype(jnp.bfloat16), norm2_w.reshape(1, -1), rw, rb)

    nt = t // TM
    ttab = ttab.reshape(nt, 8, LANES)
    nch = ttab[:, 0, :N_EXPERTS]
    cst = ttab[:, 1, :N_EXPERTS]
    used = (jnp.sum(nch, axis=1) * CH).astype(jnp.int32)
    tot = jnp.sum(nch, axis=0)
    padded = ((tot + CPT - 1) // CPT) * CPT
    gend = jnp.cumsum(padded)
    gstart = gend - padded
    seg_len = jnp.concatenate([nch.T, (padded - tot)[:, None]], axis=1).reshape(-1)
    seg_dst = jnp.cumsum(seg_len) - seg_len
    src = jnp.arange(nt, dtype=jnp.int32)[:, None] * CPL + cst
    seg_src = jnp.concatenate([src.T, jnp.zeros((N_EXPERTS, 1), jnp.int32)], axis=1).reshape(-1)
    ntiles = (TOP_K * t // CH + nt * N_EXPERTS + CPT - 1) // CPT + N_EXPERTS
    d = jnp.arange(ntiles * CPT, dtype=jnp.int32)
    off = seg_src - seg_dst
    delta = off - jnp.concatenate([jnp.zeros((1,), off.dtype), off[:-1]])
    tab = d + jnp.sum(jnp.where(seg_dst[None, :] <= d[:, None], delta[None, :], 0), axis=1)
    tab = tab.astype(jnp.int32)
    cstart_tile = jnp.arange(ntiles, dtype=jnp.int32) * CPT
    te = jnp.sum((gend[None, :] <= cstart_tile[:, None]).astype(jnp.int32), axis=1)
    te = jnp.minimum(te, N_EXPERTS - 1)
    nv = jnp.clip(gstart[te] + tot[te] - cstart_tile, 0, CPT)
    nv = jnp.where(cstart_tile < gend[-1], nv, 0).astype(jnp.int32)
    te = jnp.where(cstart_tile < gend[-1], te, te[jnp.maximum(gend[-1] // CPT - 1, 0)])
    te = te.astype(jnp.int32)

    y_loc = _experts(x_loc, tab, te, nv, exp_w1, exp_b1.reshape(N_EXPERTS, 1, 2 * D_FF),
                     exp_w2, exp_b2.reshape(N_EXPERTS, 1, D_MODEL))
    y_p, y_s = _combine(h, route, used, y_loc, final_norm_w.reshape(1, -1), tp, ts)
    return y_p.reshape(bp, lp, D_MODEL), y_s.reshape(bs, ls, D_MODEL)


def kernel(x_prompt, x_sample, norm1_w, w_in, gla_w2_f, gla_b_f, gla_w2_b, gla_b_b, gla_norm_w, sg_norm_w, sg_norm_b, sg_w, sg_b, w_out, norm2_w, router_w, router_b, exp_w1, exp_b1, exp_w2, exp_b2, final_norm_w):
    return _forward(x_prompt, x_sample, norm1_w, w_in, gla_w2_f, gla_b_f, gla_w2_b, gla_b_b,
                    gla_norm_w, sg_norm_w, sg_norm_b, sg_w, sg_b, w_out, norm2_w, router_w,
                    router_b, exp_w1, exp_b1, exp_w2, exp_b2, final_norm_w)
```

```python
import functools

import jax
import jax.numpy as jnp
from jax import lax
from jax.experimental import pallas as pl
from jax.experimental.pallas import tpu as pltpu

D_MODEL = 1024
GLA_HEADS = 4
GLA_DK = 64
GLA_DV = 128
GATE_RANK = 16
GATE_NORM = 16.0
GLA_CHUNK = 64
SG_GROUPS = 4
SG_CH = 128
SG_CHUNK = 128
N_EXPERTS = 32
TOP_K = 4
D_FF = 1024
SWIGLU_LIMIT = 7.0
SWIGLU_ALPHA = 1.702
EPS = 1e-5

LANES = 128
VMEM_LIMIT = 56 * 1024 * 1024

HEAD_W = GLA_HEADS * GLA_DV
KEY_W = GLA_HEADS * GLA_DK
HEAD_PAIRS = KEY_W // LANES
Z_Q, Z_K, Z_V = 0, KEY_W, 2 * KEY_W
QKV_W = 2 * KEY_W + HEAD_W
ZC_R, ZC_U, ZC_VS = 2, 3, 4
ZC_GATE_BLK = (QKV_W + 3 * HEAD_W) // LANES
ZC = QKV_W + 3 * HEAD_W + LANES

TM = 512
GLA_G = 512
TRI_W = 256
TM_E = 512
CH = 16
CPT = TM_E // CH
R_LOC = ((TOP_K * TM + N_EXPERTS * (CH - 1) + TM - 1) // TM) * TM
CPL = R_LOC // CH
PB = 256

NEG = -1e30


def _cparams(sem):
    return pltpu.CompilerParams(dimension_semantics=sem, vmem_limit_bytes=VMEM_LIMIT)


def _inproj_kernel(np_tiles, xp_ref, xs_ref, n1_ref, w_ref, z_ref):
    i = pl.program_id(0)
    x = jnp.where(i < np_tiles, xp_ref[...], xs_ref[...])
    ms = jnp.mean(x * x, axis=-1, keepdims=True)
    xn = (x * lax.rsqrt(ms + EPS)) * n1_ref[...]
    z = jnp.dot(xn.astype(jnp.bfloat16), w_ref[...], preferred_element_type=jnp.float32)
    z_ref[...] = z.astype(jnp.bfloat16)


def _inproj(xp, xs, n1, w_all):
    tp, ts = xp.shape[0], xs.shape[0]
    npt, nst = tp // TM, ts // TM
    return pl.pallas_call(
        functools.partial(_inproj_kernel, npt),
        grid=(npt + nst,),
        in_specs=[
            pl.BlockSpec((TM, D_MODEL), lambda i: (jnp.minimum(i, npt - 1), 0)),
            pl.BlockSpec((TM, D_MODEL), lambda i: (jnp.maximum(i - npt, 0), 0)),
            pl.BlockSpec((1, D_MODEL), lambda i: (0, 0)),
            pl.BlockSpec((D_MODEL, ZC), lambda i: (0, 0)),
        ],
        out_specs=pl.BlockSpec((TM, ZC), lambda i: (i, 0)),
        out_shape=jax.ShapeDtypeStruct((tp + ts, ZC), jnp.bfloat16),
        compiler_params=_cparams(("arbitrary",)),
        name="inproj",
    )(xp, xs, n1, w_all)


def _log_sigmoid(x):
    return jnp.minimum(x, 0.0) - jnp.log1p(jnp.exp(-jnp.abs(x)))


_NT = (((1,), (1,)), ((), ()))
_TN = (((0,), (0,)), ((), ()))


def _log_decay(gz_ref, w2_ref, gb_ref, tri_ref):
    gpre = jnp.dot(gz_ref[...], w2_ref[...], preferred_element_type=jnp.float32) + gb_ref[...]
    g = _log_sigmoid(gpre) * (1.0 / GATE_NORM)
    g_hi = g.astype(jnp.bfloat16)
    r1 = g - g_hi.astype(jnp.float32)
    g_mid = r1.astype(jnp.bfloat16)
    g_lo = (r1 - g_mid.astype(jnp.float32)).astype(jnp.bfloat16)
    tri = tri_ref[...]
    parts = []
    for s0 in range(0, GLA_G, TRI_W):
        blk = slice(s0, s0 + TRI_W)
        parts.append(jnp.dot(tri, g_hi[blk], preferred_element_type=jnp.float32)
                     + jnp.dot(tri, g_mid[blk], preferred_element_type=jnp.float32)
                     + jnp.dot(tri, g_lo[blk], preferred_element_type=jnp.float32))
    return jnp.concatenate(parts, axis=0)


def _gla_direction(z_ref, b, mask2, st_ref, o_ref, reverse):
    nchunk = GLA_G // GLA_CHUNK
    order = range(nchunk - 1, -1, -1) if reverse else range(nchunk)
    first = lax.broadcasted_iota(jnp.int32, (1, LANES), 1) < GLA_DK
    for c in order:
        r0 = c * GLA_CHUNK
        last = r0 if reverse else r0 + GLA_CHUNK - 1
        rows = slice(r0, r0 + GLA_CHUNK)
        for p in range(HEAD_PAIRS):
            l0 = p * LANES
            bp = b[rows, l0:l0 + LANES]
            bl = b[last:last + 1, l0:l0 + LANES]
            qp = z_ref[rows, Z_Q + l0:Z_Q + l0 + LANES].astype(jnp.float32)
            kp = z_ref[rows, Z_K + l0:Z_K + l0 + LANES].astype(jnp.float32)
            v0 = Z_V + 2 * p * GLA_DV
            vpair = z_ref[rows, v0:v0 + 2 * GLA_DV]
            qt = qp * jnp.exp(bp)
            kt = (kp * jnp.exp(-bp)).astype(jnp.bfloat16)
            kd = (kp * jnp.exp(bl - bp)).astype(jnp.bfloat16)
            q2 = jnp.concatenate([jnp.where(first, qt, 0.0), jnp.where(first, 0.0, qt)],
                                 axis=0).astype(jnp.bfloat16)
            a2 = (lax.dot_general(q2, kt, _NT, preferred_element_type=jnp.float32)
                  * mask2).astype(jnp.bfloat16)
            st = st_ref[p]
            inter = lax.dot_general(q2, st.astype(jnp.bfloat16), _NT,
                                    preferred_element_type=jnp.float32)
            for hh in range(2):
                hr = slice(hh * GLA_CHUNK, (hh + 1) * GLA_CHUNK)
                vh = vpair[:, hh * GLA_DV:(hh + 1) * GLA_DV]
                o = jnp.dot(a2[hr], vh, preferred_element_type=jnp.float32) + inter[hr]
                oc = (2 * p + hh) * GLA_DV
                o_ref[rows, oc:oc + GLA_DV] = o
            u2 = lax.dot_general(vpair, kd, _TN, preferred_element_type=jnp.float32)
            st_ref[p] = st * jnp.exp(bl) + jnp.where(first, u2[:GLA_DV], u2[GLA_DV:])


def _gla_kernel(start_ref, n_ref, zf, gzf, zb, gzb,
                w2f, gbf, w2b, gbb, tril, triu, maskf, maskb,
                of_ref, ob_ref, stf, stb):
    s = pl.program_id(0)
    i = pl.program_id(1)

    @pl.when(i == 0)
    def _():
        stf[...] = jnp.zeros_like(stf)
        stb[...] = jnp.zeros_like(stb)

    @pl.when(i < n_ref[s])
    def _():
        b_f = _log_decay(gzf, w2f, gbf, tril)
        b_b = _log_decay(gzb, w2b, gbb, triu)
        _gla_direction(zf, b_f, maskf[...], stf, of_ref, False)
        _gla_direction(zb, b_b, maskb[...], stb, ob_ref, True)


def _gla(z, seq_start, seq_n, max_groups, w2f, gbf, w2b, gbb):
    t = z.shape[0]
    nseq = seq_start.shape[0]

    def fwd(s, i, st, n):
        return st[s] + jnp.minimum(i, n[s] - 1)

    def bwd(s, i, st, n):
        return st[s] + jnp.maximum(n[s] - 1 - i, 0)

    def zspec(width, col, blk):
        return pl.BlockSpec((GLA_G, width), lambda s, i, st, n: (blk(s, i, st, n), col))

    def const(shape):
        return pl.BlockSpec(shape, lambda s, i, st, n: (0,) * len(shape))

    r = lax.broadcasted_iota(jnp.int32, (TRI_W, TRI_W), 0)
    c = lax.broadcasted_iota(jnp.int32, (TRI_W, TRI_W), 1)
    same = (r // GLA_CHUNK) == (c // GLA_CHUNK)
    tril = (same & (c <= r)).astype(jnp.bfloat16)
    triu = (same & (c >= r)).astype(jnp.bfloat16)
    rr = lax.broadcasted_iota(jnp.int32, (GLA_CHUNK, GLA_CHUNK), 0)
    cc = lax.broadcasted_iota(jnp.int32, (GLA_CHUNK, GLA_CHUNK), 1)
    maskf = (cc <= rr).astype(jnp.float32)
    maskb = (cc > rr).astype(jnp.float32)
    maskf = jnp.concatenate([maskf, maskf], axis=0)
    maskb = jnp.concatenate([maskb, maskb], axis=0)

    in_specs = [
        zspec(QKV_W, 0, fwd), zspec(LANES, ZC_GATE_BLK, fwd),
        zspec(QKV_W, 0, bwd), zspec(LANES, ZC_GATE_BLK, bwd),
        const((LANES, KEY_W)), const((1, KEY_W)), const((LANES, KEY_W)), const((1, KEY_W)),
        const((TRI_W, TRI_W)), const((TRI_W, TRI_W)),
        const((2 * GLA_CHUNK, GLA_CHUNK)), const((2 * GLA_CHUNK, GLA_CHUNK)),
    ]
    out_specs = [
        pl.BlockSpec((GLA_G, HEAD_W), lambda s, i, st, n: (fwd(s, i, st, n), 0)),
        pl.BlockSpec((GLA_G, HEAD_W), lambda s, i, st, n: (bwd(s, i, st, n), 0)),
    ]
    grid_spec = pltpu.PrefetchScalarGridSpec(
        num_scalar_prefetch=2,
        grid=(nseq, max_groups),
        in_specs=in_specs,
        out_specs=out_specs,
        scratch_shapes=[pltpu.VMEM((HEAD_PAIRS, GLA_DV, LANES), jnp.float32),
                        pltpu.VMEM((HEAD_PAIRS, GLA_DV, LANES), jnp.float32)],
    )
    return pl.pallas_call(
        _gla_kernel,
        grid_spec=grid_spec,
        out_shape=[jax.ShapeDtypeStruct((t, HEAD_W), jnp.float32),
                   jax.ShapeDtypeStruct((t, HEAD_W), jnp.float32)],
        compiler_params=_cparams(("arbitrary", "arbitrary")),
        name="gla",
    )(seq_start, seq_n, z, z, z, z, w2f, gbf, w2b, gbb, tril, triu, maskf, maskb)


def _gelu(x):
    return 0.5 * x * (1.0 + lax.erf(x * (2.0 ** -0.5)))


def _mixer_router_kernel(np_tiles, of_ref, ob_ref, r_ref, u_ref, vs_ref, xp_ref, xs_ref,
                         gnw_ref, snw_ref, snb_ref, sgw_ref, sgb_ref, wout_ref, n2_ref,
                         rw_ref, rb_ref, tri_ref, upper_ref,
                         h_ref, xloc_ref, route_ref, ttab_ref, lrow_scr, hn_scr):
    i = pl.program_id(0)

    @pl.when(i == 0)
    def _():
        lrow_scr[...] = jnp.zeros_like(lrow_scr)
        hn_scr[...] = jnp.zeros_like(hn_scr)

    rid = lax.broadcasted_iota(jnp.int32, (PB, TM), 0).astype(jnp.float32).astype(jnp.bfloat16)
    one = jnp.ones((PB, TM), jnp.bfloat16)
    zero = jnp.zeros((PB, TM), jnp.bfloat16)

    def place(blocks):
        for rb in blocks:
            hit = None
            for k in range(TOP_K):
                lp = (lrow_scr[k:k + 1, :] - float(rb * PB)).astype(jnp.bfloat16)
                hk = rid == lp
                hit = hk if hit is None else hit | hk
            xloc_ref[rb * PB:(rb + 1) * PB, :] = jnp.dot(
                jnp.where(hit, one, zero), hn_scr[...],
                preferred_element_type=jnp.float32).astype(jnp.bfloat16)

    place((0,))
    o = of_ref[...] + ob_ref[...]
    parts = []
    for hd in range(GLA_HEADS):
        oh = o[:, hd * GLA_DV:(hd + 1) * GLA_DV]
        ms = jnp.mean(oh * oh, axis=-1, keepdims=True)
        parts.append(oh * lax.rsqrt(ms + EPS))
    on = jnp.concatenate(parts, axis=1) * gnw_ref[...]
    r = r_ref[...].astype(jnp.float32)
    gla_out = on * (r / (1.0 + jnp.exp(-r)))

    place((1,))
    u = _gelu(u_ref[...].astype(jnp.float32))
    place((2,))
    vf = _gelu(vs_ref[...].astype(jnp.float32))
    place((3,))
    parts = []
    for g in range(SG_GROUPS):
        vg = vf[:, g * SG_CH:(g + 1) * SG_CH]
        mu = jnp.mean(vg, axis=-1, keepdims=True)
        vc = vg - mu
        var = jnp.mean(vc * vc, axis=-1, keepdims=True)
        parts.append(vc * lax.rsqrt(var + EPS))
    vn = (jnp.concatenate(parts, axis=1) * snw_ref[...] + snb_ref[...]).astype(jnp.bfloat16)
    nch = TM // SG_CHUNK
    res = []
    for g in range(SG_GROUPS):
        vcat = jnp.concatenate(
            [vn[c * SG_CHUNK:(c + 1) * SG_CHUNK, g * SG_CH:(g + 1) * SG_CH] for c in range(nch)],
            axis=1)
        res.append(jnp.dot(sgw_ref[g], vcat, preferred_element_type=jnp.float32))
    bias = sgb_ref[...]
    rows = []
    for c in range(nch):
        rows.append(jnp.concatenate(
            [res[g][:, c * SG_CH:(c + 1) * SG_CH] for g in range(SG_GROUPS)], axis=1) + bias)
    sg_out = u * jnp.concatenate(rows, axis=0)

    mixed = jnp.concatenate([gla_out, sg_out], axis=1).astype(jnp.bfloat16)
    x = jnp.where(i < np_tiles, xp_ref[...], xs_ref[...])
    h = x + jnp.dot(mixed, wout_ref[...], preferred_element_type=jnp.float32)
    h_ref[...] = h

    ms = jnp.mean(h * h, axis=-1, keepdims=True)
    hn = (h * lax.rsqrt(ms + EPS)) * n2_ref[...]
    hn_hi = hn.astype(jnp.bfloat16)
    hn_lo = (hn - hn_hi.astype(jnp.float32)).astype(jnp.bfloat16)
    lg2 = jnp.dot(hn_hi, rw_ref[...], preferred_element_type=jnp.float32)
    logits = (lg2[:, :LANES] + lg2[:, LANES:]
              + jnp.dot(hn_lo, rw_ref[:, :LANES], preferred_element_type=jnp.float32)
              + rb_ref[...])
    lane = lax.broadcasted_iota(jnp.int32, logits.shape, 1)
    work = jnp.where(lane < N_EXPERTS, logits, NEG)
    top_v, top_i, onehots = [], [], []
    for k in range(TOP_K):
        place((4 + k,))
        m = jnp.max(work, axis=-1, keepdims=True)
        idx = jnp.min(jnp.where(work == m, lane, LANES), axis=-1, keepdims=True)
        sel = lane == idx
        top_v.append(m)
        top_i.append(idx)
        onehots.append(sel)
        work = jnp.where(sel, NEG, work)
    e = [jnp.exp(v - top_v[0]) for v in top_v]
    den = e[0] + e[1] + e[2] + e[3]
    sel_any = onehots[0] | onehots[1] | onehots[2] | onehots[3]
    msel = sel_any.astype(jnp.bfloat16)
    before = jnp.dot(tri_ref[...], msel, preferred_element_type=jnp.float32)
    cnt = jnp.sum(sel_any.astype(jnp.float32), axis=0, keepdims=True)
    nchunk = jnp.floor((cnt + (CH - 1)) * (1.0 / CH))
    nchunk8 = jnp.broadcast_to(nchunk, (8, LANES)).astype(jnp.bfloat16)
    cstart = jnp.dot(nchunk8, upper_ref[...], preferred_element_type=jnp.float32)
    base = before + cstart[0:1, :] * float(CH)
    place(range(4 + TOP_K, R_LOC // PB))
    route = jnp.zeros(logits.shape, jnp.float32)
    for k in range(TOP_K):
        lpos = jnp.sum(jnp.where(onehots[k], base, 0.0), axis=-1, keepdims=True)
        route = jnp.where(lane == k, lpos, route)
        route = jnp.where(lane == TOP_K + k, e[k] / den, route)
    route_ref[...] = route
    sub = lax.broadcasted_iota(jnp.int32, (8, LANES), 0)
    ttab = jnp.where(sub == 0, nchunk8.astype(jnp.float32), jnp.where(sub == 1, cstart, 0.0))
    ttab_ref[...] = ttab.astype(jnp.int32)
    lrow_scr[...] = jnp.transpose(route)[0:8, :]
    hn_scr[...] = hn_hi


def _mixer_router(o_f, o_b, z, xp, xs, gnw, snw, snb, sgw, sgb_full, wout, n2, rw, rb):
    t = z.shape[0]
    npt = xp.shape[0] // TM
    nt = t // TM
    tri = (lax.broadcasted_iota(jnp.int32, (TM, TM), 1)
           < lax.broadcasted_iota(jnp.int32, (TM, TM), 0)).astype(jnp.bfloat16)
    upper = (lax.broadcasted_iota(jnp.int32, (LANES, LANES), 0)
             < lax.broadcasted_iota(jnp.int32, (LANES, LANES), 1)).astype(jnp.bfloat16)

    def const(shape):
        return pl.BlockSpec(shape, lambda i: (0,) * len(shape))

    nst = nt - npt

    def tok(width, col=0):
        return pl.BlockSpec((TM, width), lambda i: (jnp.minimum(i, nt - 1), col))

    in_specs = [
        tok(HEAD_W), tok(HEAD_W),
        tok(HEAD_W, ZC_R), tok(HEAD_W, ZC_U), tok(HEAD_W, ZC_VS),
        pl.BlockSpec((TM, D_MODEL), lambda i: (jnp.minimum(i, npt - 1), 0)),
        pl.BlockSpec((TM, D_MODEL), lambda i: (jnp.clip(i - npt, 0, nst - 1), 0)),
        const((1, HEAD_W)), const((1, HEAD_W)), const((1, HEAD_W)),
        const((SG_GROUPS, SG_CHUNK, SG_CHUNK)), const((SG_CHUNK, HEAD_W)),
        const((D_MODEL, D_MODEL)), const((1, D_MODEL)),
        const((D_MODEL, 2 * LANES)), const((1, LANES)), const((TM, TM)), const((LANES, LANES)),
    ]
    out_specs = [
        tok(D_MODEL),
        pl.BlockSpec((R_LOC, D_MODEL), lambda i: (jnp.maximum(i - 1, 0), 0)),
        tok(LANES),
        pl.BlockSpec((8, LANES), lambda i: (jnp.minimum(i, nt - 1), 0)),
    ]
    out_shape = [
        jax.ShapeDtypeStruct((t, D_MODEL), jnp.float32),
        jax.ShapeDtypeStruct((nt * R_LOC, D_MODEL), jnp.bfloat16),
        jax.ShapeDtypeStruct((t, LANES), jnp.float32),
        jax.ShapeDtypeStruct((nt * 8, LANES), jnp.int32),
    ]
    return pl.pallas_call(
        functools.partial(_mixer_router_kernel, npt),
        grid=(nt + 1,),
        in_specs=in_specs,
        out_specs=out_specs,
        out_shape=out_shape,
        scratch_shapes=[pltpu.VMEM((8, TM), jnp.float32),
                        pltpu.VMEM((TM, D_MODEL), jnp.bfloat16)],
        compiler_params=_cparams(("arbitrary",)),
        name="mixer_router",
    )(o_f, o_b, z, z, z, xp, xs, gnw, snw, snb, sgw, sgb_full, wout, n2, rw, rb, tri, upper)


def _chunk_rows(chunk):
    start = chunk * CH
    return pl.ds(start if isinstance(chunk, int) else pl.multiple_of(start, CH), CH)


def _chunk_copy(src_ref, src_chunk, dst_ref, dst_chunk, sem):
    return pltpu.make_async_copy(src_ref.at[_chunk_rows(src_chunk)],
                                 dst_ref.at[_chunk_rows(dst_chunk)], sem)


def _expert_kernel(tab_ref, te_ref, nv_ref, xloc_ref, w1_ref, b1_ref, w2_ref, b2_ref,
                   yloc_ref, xbuf, ybuf, w1b, w2b, gsem, ssem):
    j = pl.program_id(0)
    last = pl.num_programs(0) - 1
    slot = lax.rem(j, 2)
    nv = nv_ref[j]

    def move(tile, start, chunk_copy, full_copy):
        n = nv_ref[tile]

        @pl.when(n == CPT)
        def _():
            if start:
                for c in range(CPT):
                    chunk_copy(c).start()
            else:
                full_copy().wait()

        @pl.when(n < CPT)
        def _():
            def body(c, carry):
                if start:
                    chunk_copy(c).start()
                else:
                    chunk_copy(c).wait()
                return carry
            lax.fori_loop(0, n, body, 0)

    def gather(tile, buf_slot, start):
        move(tile, start,
             lambda c: _chunk_copy(xloc_ref, tab_ref[tile * CPT + c], xbuf.at[buf_slot], c,
                                   gsem.at[buf_slot]),
             lambda: pltpu.make_async_copy(xloc_ref.at[pl.ds(0, TM_E)], xbuf.at[buf_slot],
                                           gsem.at[buf_slot]))

    def scatter(tile, buf_slot, start):
        move(tile, start,
             lambda c: _chunk_copy(ybuf.at[buf_slot], c, yloc_ref, tab_ref[tile * CPT + c],
                                   ssem.at[buf_slot]),
             lambda: pltpu.make_async_copy(ybuf.at[buf_slot], yloc_ref.at[pl.ds(0, TM_E)],
                                           ssem.at[buf_slot]))

    @pl.when(j == 0)
    def _():
        gather(0, 0, True)

    gather(j, slot, False)

    @pl.when(j < last)
    def _():
        gather(j + 1, 1 - slot, True)

    prev = te_ref[jnp.maximum(j - 1, 0)]

    @pl.when((j == 0) | (te_ref[j] != prev))
    def _():
        w1b[...] = w1_ref[0].astype(jnp.bfloat16)
        w2b[...] = w2_ref[0].astype(jnp.bfloat16)

    @pl.when(j >= 2)
    def _():
        scatter(j - 2, slot, False)

    @pl.when(nv > 0)
    def _():
        row = lax.broadcasted_iota(jnp.int32, (TM_E, 1), 0)
        x = xbuf[slot]
        x = jnp.where(row < nv * CH, x, jnp.zeros_like(x))
        zz = jnp.dot(x, w1b[...], preferred_element_type=jnp.float32) + b1_ref[0]
        gate = jnp.minimum(zz[:, :D_FF], SWIGLU_LIMIT)
        lin = jnp.clip(zz[:, D_FF:], -SWIGLU_LIMIT, SWIGLU_LIMIT)
        a = gate * (1.0 / (1.0 + jnp.exp(-SWIGLU_ALPHA * gate))) * (lin + 1.0)
        y = jnp.dot(a.astype(jnp.bfloat16), w2b[...],
                    preferred_element_type=jnp.float32) + b2_ref[0]
        ybuf[slot] = y.astype(jnp.bfloat16)
        scatter(j, slot, True)

    @pl.when(j == last)
    def _():
        @pl.when(j >= 1)
        def _():
            scatter(j - 1, 1 - slot, False)
        scatter(j, slot, False)


def _experts(x_loc, tab, te, nv, w1, b1, w2, b2):
    ntiles = te.shape[0]
    grid_spec = pltpu.PrefetchScalarGridSpec(
        num_scalar_prefetch=3,
        grid=(ntiles,),
        in_specs=[
            pl.BlockSpec(memory_space=pl.ANY),
            pl.BlockSpec((1, D_MODEL, 2 * D_FF), lambda j, tab, te, nv: (te[j], 0, 0)),
            pl.BlockSpec((1, 1, 2 * D_FF), lambda j, tab, te, nv: (te[j], 0, 0)),
            pl.BlockSpec((1, D_FF, D_MODEL), lambda j, tab, te, nv: (te[j], 0, 0)),
            pl.BlockSpec((1, 1, D_MODEL), lambda j, tab, te, nv: (te[j], 0, 0)),
        ],
        out_specs=pl.BlockSpec(memory_space=pl.ANY),
        scratch_shapes=[pltpu.VMEM((2, TM_E, D_MODEL), jnp.bfloat16),
                        pltpu.VMEM((2, TM_E, D_MODEL), jnp.bfloat16),
                        pltpu.VMEM((D_MODEL, 2 * D_FF), jnp.bfloat16),
                        pltpu.VMEM((D_FF, D_MODEL), jnp.bfloat16),
                        pltpu.SemaphoreType.DMA((2,)),
                        pltpu.SemaphoreType.DMA((2,))],
    )
    return pl.pallas_call(
        _expert_kernel,
        grid_spec=grid_spec,
        out_shape=jax.ShapeDtypeStruct(x_loc.shape, jnp.bfloat16),
        input_output_aliases={3: 0},
        compiler_params=_cparams(("arbitrary",)),
        name="experts",
    )(tab, te, nv, x_loc, w1, b1, w2, b2)


def _combine_kernel(np_tiles, h_ref, route_ref, fw_ref, yloc_ref, outp_ref, outs_ref):
    i = pl.program_id(0)
    route = route_ref[...]
    cid = lax.broadcasted_iota(jnp.int32, (TM, LANES), 1).astype(jnp.float32)
    zero = jnp.zeros((TM, LANES), jnp.bfloat16)
    lpos = [jnp.broadcast_to(route[:, k:k + 1], (TM, LANES)) for k in range(TOP_K)]
    gates = [jnp.broadcast_to(route[:, TOP_K + k:TOP_K + k + 1], (TM, LANES)).astype(jnp.bfloat16)
             for k in range(TOP_K)]
    acc = h_ref[...]
    for cb in range(R_LOC // PB):
        lps = [(lpos[k] - float(cb * PB)).astype(jnp.bfloat16) for k in range(TOP_K)]
        tiles = []
        for lt in range(PB // LANES):
            cl = (cid + float(lt * LANES)).astype(jnp.bfloat16)
            w = None
            for k in range(TOP_K):
                wk = jnp.where(cl == lps[k], gates[k], zero)
                w = wk if w is None else w + wk
            tiles.append(w)
        acc = acc + jnp.dot(jnp.concatenate(tiles, axis=1), yloc_ref[cb * PB:(cb + 1) * PB, :],
                            preferred_element_type=jnp.float32)
    ms = jnp.mean(acc * acc, axis=-1, keepdims=True)
    res = (acc * lax.rsqrt(ms + EPS)) * fw_ref[...]

    @pl.when(i < np_tiles)
    def _():
        outp_ref[...] = res

    @pl.when(i >= np_tiles)
    def _():
        outs_ref[...] = res


def _combine(h, route, y_loc, fw, tp, ts):
    t = h.shape[0]
    npt = tp // TM
    nt = t // TM
    return pl.pallas_call(
        functools.partial(_combine_kernel, npt),
        grid=(nt,),
        in_specs=[
            pl.BlockSpec((TM, D_MODEL), lambda i: (i, 0)),
            pl.BlockSpec((TM, LANES), lambda i: (i, 0)),
            pl.BlockSpec((1, D_MODEL), lambda i: (0, 0)),
            pl.BlockSpec((R_LOC, D_MODEL), lambda i: (i, 0)),
        ],
        out_specs=[
            pl.BlockSpec((TM, D_MODEL), lambda i: (jnp.minimum(i, npt - 1), 0)),
            pl.BlockSpec((TM, D_MODEL), lambda i: (jnp.maximum(i - npt, 0), 0)),
        ],
        out_shape=[jax.ShapeDtypeStruct((tp, D_MODEL), jnp.float32),
                   jax.ShapeDtypeStruct((ts, D_MODEL), jnp.float32)],
        compiler_params=_cparams(("arbitrary",)),
        name="combine",
    )(h, route, fw, y_loc)


def _forward(x_prompt, x_sample, norm1_w, w_in, gla_w2_f, gla_b_f, gla_w2_b, gla_b_b,
             gla_norm_w, sg_norm_w, sg_norm_b, sg_w, sg_b, w_out, norm2_w, router_w,
             router_b, exp_w1, exp_b1, exp_w2, exp_b2, final_norm_w):
    bp, lp, _ = x_prompt.shape
    bs, ls, _ = x_sample.shape
    tp, ts = bp * lp, bs * ls
    t = tp + ts
    assert lp % TM == 0 and ls % TM == 0 and TM % GLA_G == 0
    xp = x_prompt.reshape(tp, D_MODEL)
    xs = x_sample.reshape(ts, D_MODEL)

    g0 = QKV_W + HEAD_W
    u0 = g0 + 2 * GATE_RANK
    wgate = jnp.pad(w_in[:, g0:u0], ((0, 0), (0, LANES - 2 * GATE_RANK)))
    w_all = jnp.concatenate(
        [w_in[:, :KEY_W] * (GLA_DK ** -0.5), w_in[:, KEY_W:g0], w_in[:, u0:], wgate], axis=1
    ).astype(jnp.bfloat16)

    z = _inproj(xp, xs, norm1_w.reshape(1, D_MODEL), w_all)

    w2f = jnp.zeros((LANES, KEY_W), jnp.float32).at[:GATE_RANK].set(gla_w2_f)
    w2b = jnp.zeros((LANES, KEY_W), jnp.float32).at[GATE_RANK:2 * GATE_RANK].set(gla_w2_b)
    gbf = gla_b_f.reshape(1, -1)
    gbb = gla_b_b.reshape(1, -1)
    gp, gs = lp // GLA_G, ls // GLA_G
    seq_start = jnp.concatenate([jnp.arange(bp, dtype=jnp.int32) * gp,
                                 bp * gp + jnp.arange(bs, dtype=jnp.int32) * gs])
    seq_n = jnp.concatenate([jnp.full((bp,), gp, jnp.int32), jnp.full((bs,), gs, jnp.int32)])
    o_f, o_b = _gla(z, seq_start, seq_n, max(gp, gs), w2f.astype(jnp.bfloat16), gbf,
                    w2b.astype(jnp.bfloat16), gbb)

    sgb_full = jnp.repeat(sg_b.T, SG_CH, axis=1)
    rw = jnp.pad(router_w, ((0, 0), (0, LANES - N_EXPERTS)))
    rw_hi = rw.astype(jnp.bfloat16)
    rw_lo = (rw - rw_hi.astype(jnp.float32)).astype(jnp.bfloat16)
    rw = jnp.concatenate([rw_hi, rw_lo], axis=1)
    rb = jnp.pad(router_b.reshape(1, -1), ((0, 0), (0, LANES - N_EXPERTS)))
    h, x_loc, route, ttab = _mixer_router(
        o_f, o_b, z, xp, xs, gla_norm_w.reshape(1, -1), sg_norm_w.reshape(1, -1),
        sg_norm_b.reshape(1, -1), sg_w.astype(jnp.bfloat16), sgb_full,
        w_out.astype(jnp.bfloat16), norm2_w.reshape(1, -1), rw, rb)

    nt = t // TM
    ttab = ttab.reshape(nt, 8, LANES)
    nch = ttab[:, 0, :N_EXPERTS]
    cst = ttab[:, 1, :N_EXPERTS]
    tot = jnp.sum(nch, axis=0)
    padded = ((tot + CPT - 1) // CPT) * CPT
    gend = jnp.cumsum(padded)
    gstart = gend - padded
    seg_len = jnp.concatenate([nch.T, (padded - tot)[:, None]], axis=1).reshape(-1)
    seg_dst = jnp.cumsum(seg_len) - seg_len
    src = jnp.arange(nt, dtype=jnp.int32)[:, None] * CPL + cst
    seg_src = jnp.concatenate([src.T, jnp.zeros((N_EXPERTS, 1), jnp.int32)], axis=1).reshape(-1)
    ntiles = (TOP_K * t // CH + nt * N_EXPERTS + CPT - 1) // CPT + N_EXPERTS
    d = jnp.arange(ntiles * CPT, dtype=jnp.int32)
    off = seg_src - seg_dst
    delta = off - jnp.concatenate([jnp.zeros((1,), off.dtype), off[:-1]])
    tab = d + jnp.sum(jnp.where(seg_dst[None, :] <= d[:, None], delta[None, :], 0), axis=1)
    tab = tab.astype(jnp.int32)
    cstart_tile = jnp.arange(ntiles, dtype=jnp.int32) * CPT
    te = jnp.sum((gend[None, :] <= cstart_tile[:, None]).astype(jnp.int32), axis=1)
    te = jnp.minimum(te, N_EXPERTS - 1)
    nv = jnp.clip(gstart[te] + tot[te] - cstart_tile, 0, CPT)
    nv = jnp.where(cstart_tile < gend[-1], nv, 0).astype(jnp.int32)
    te = jnp.where(cstart_tile < gend[-1], te, te[jnp.maximum(gend[-1] // CPT - 1, 0)])
    te = te.astype(jnp.int32)

    y_loc = _experts(x_loc, tab, te, nv, exp_w1, exp_b1.reshape(N_EXPERTS, 1, 2 * D_FF),
                     exp_w2, exp_b2.reshape(N_EXPERTS, 1, D_MODEL))
    y_p, y_s = _combine(h, route, y_loc, final_norm_w.reshape(1, -1), tp, ts)
    return y_p.reshape(bp, lp, D_MODEL), y_s.reshape(bs, ls, D_MODEL)


def kernel(x_prompt, x_sample, norm1_w, w_in, gla_w2_f, gla_b_f, gla_w2_b, gla_b_b, gla_norm_w, sg_norm_w, sg_norm_b, sg_w, sg_b, w_out, norm2_w, router_w, router_b, exp_w1, exp_b1, exp_w2, exp_b2, final_norm_w):
    return _forward(x_prompt, x_sample, norm1_w, w_in, gla_w2_f, gla_b_f, gla_w2_b, gla_b_b,
                    gla_norm_w, sg_norm_w, sg_norm_b, sg_w, sg_b, w_out, norm2_w, router_w,
                    router_b, exp_w1, exp_b1, exp_w2, exp_b2, final_norm_w)
```

```python
import functools

import jax
import jax.numpy as jnp
from jax import lax
from jax.experimental import pallas as pl
from jax.experimental.pallas import tpu as pltpu

D_MODEL = 1024
GLA_HEADS = 4
GLA_DK = 64
GLA_DV = 128
GATE_RANK = 16
GATE_NORM = 16.0
GLA_CHUNK = 64
SG_GROUPS = 4
SG_CH = 128
SG_CHUNK = 128
N_EXPERTS = 32
TOP_K = 4
D_FF = 1024
SWIGLU_LIMIT = 7.0
SWIGLU_ALPHA = 1.702
EPS = 1e-5

LANES = 128
VMEM_LIMIT = 56 * 1024 * 1024

HEAD_W = GLA_HEADS * GLA_DV
KEY_W = GLA_HEADS * GLA_DK
HEAD_PAIRS = KEY_W // LANES
Z_Q, Z_K, Z_V = 0, KEY_W, 2 * KEY_W
QKV_W = 2 * KEY_W + HEAD_W
ZC_R, ZC_U, ZC_VS = 2, 3, 4
ZC_GATE_BLK = (QKV_W + 3 * HEAD_W) // LANES
ZC = QKV_W + 3 * HEAD_W + LANES

TM = 512
GLA_G = 512
TRI_W = 256
TM_E = 512
CH = 16
CPT = TM_E // CH
R_LOC = ((TOP_K * TM + N_EXPERTS * (CH - 1) + TM - 1) // TM) * TM
CPL = R_LOC // CH
PB = 256

NEG = -1e30


def _cparams(sem):
    return pltpu.CompilerParams(dimension_semantics=sem, vmem_limit_bytes=VMEM_LIMIT)


def _inproj_kernel(np_tiles, xp_ref, xs_ref, n1_ref, w_ref, z_ref):
    i = pl.program_id(0)
    x = jnp.where(i < np_tiles, xp_ref[...], xs_ref[...])
    ms = jnp.mean(x * x, axis=-1, keepdims=True)
    xn = (x * lax.rsqrt(ms + EPS)) * n1_ref[...]
    z = jnp.dot(xn.astype(jnp.bfloat16), w_ref[...], preferred_element_type=jnp.float32)
    z_ref[...] = z.astype(jnp.bfloat16)


def _inproj(xp, xs, n1, w_all):
    tp, ts = xp.shape[0], xs.shape[0]
    npt, nst = tp // TM, ts // TM
    return pl.pallas_call(
        functools.partial(_inproj_kernel, npt),
        grid=(npt + nst,),
        in_specs=[
            pl.BlockSpec((TM, D_MODEL), lambda i: (jnp.minimum(i, npt - 1), 0)),
            pl.BlockSpec((TM, D_MODEL), lambda i: (jnp.maximum(i - npt, 0), 0)),
            pl.BlockSpec((1, D_MODEL), lambda i: (0, 0)),
            pl.BlockSpec((D_MODEL, ZC), lambda i: (0, 0)),
        ],
        out_specs=pl.BlockSpec((TM, ZC), lambda i: (i, 0)),
        out_shape=jax.ShapeDtypeStruct((tp + ts, ZC), jnp.bfloat16),
        compiler_params=_cparams(("arbitrary",)),
        name="inproj",
    )(xp, xs, n1, w_all)


def _log_sigmoid(x):
    return jnp.minimum(x, 0.0) - jnp.log1p(jnp.exp(-jnp.abs(x)))


_NT = (((1,), (1,)), ((), ()))
_TN = (((0,), (0,)), ((), ()))


def _log_decay(gz_ref, w2_ref, gb_ref, tri_ref):
    gpre = jnp.dot(gz_ref[...], w2_ref[...], preferred_element_type=jnp.float32) + gb_ref[...]
    g = _log_sigmoid(gpre) * (1.0 / GATE_NORM)
    g_hi = g.astype(jnp.bfloat16)
    r1 = g - g_hi.astype(jnp.float32)
    g_mid = r1.astype(jnp.bfloat16)
    g_lo = (r1 - g_mid.astype(jnp.float32)).astype(jnp.bfloat16)
    tri = tri_ref[...]
    parts = []
    for s0 in range(0, GLA_G, TRI_W):
        blk = slice(s0, s0 + TRI_W)
        parts.append(jnp.dot(tri, g_hi[blk], preferred_element_type=jnp.float32)
                     + jnp.dot(tri, g_mid[blk], preferred_element_type=jnp.float32)
                     + jnp.dot(tri, g_lo[blk], preferred_element_type=jnp.float32))
    return jnp.concatenate(parts, axis=0)


def _gla_chunk(z_ref, b, mask2, st_ref, o_ref, reverse, step):
    nchunk = GLA_G // GLA_CHUNK
    first = lax.broadcasted_iota(jnp.int32, (1, LANES), 1) < GLA_DK
    for c in (nchunk - 1 - step if reverse else step,):
        r0 = c * GLA_CHUNK
        last = r0 if reverse else r0 + GLA_CHUNK - 1
        rows = slice(r0, r0 + GLA_CHUNK)
        for p in range(HEAD_PAIRS):
            l0 = p * LANES
            bp = b[rows, l0:l0 + LANES]
            bl = b[last:last + 1, l0:l0 + LANES]
            qp = z_ref[rows, Z_Q + l0:Z_Q + l0 + LANES].astype(jnp.float32)
            kp = z_ref[rows, Z_K + l0:Z_K + l0 + LANES].astype(jnp.float32)
            v0 = Z_V + 2 * p * GLA_DV
            vpair = z_ref[rows, v0:v0 + 2 * GLA_DV]
            qt = qp * jnp.exp(bp)
            kt = (kp * jnp.exp(-bp)).astype(jnp.bfloat16)
            kd = kp * jnp.exp(bl - bp)
            q2 = jnp.concatenate([jnp.where(first, qt, 0.0), jnp.where(first, 0.0, qt)],
                                 axis=0).astype(jnp.bfloat16)
            a2 = (lax.dot_general(q2, kt, _NT, preferred_element_type=jnp.float32)
                  * mask2).astype(jnp.bfloat16)
            st = st_ref[p]
            inter = jnp.dot(q2, jnp.transpose(st).astype(jnp.bfloat16),
                            preferred_element_type=jnp.float32)
            for hh in range(2):
                hr = slice(hh * GLA_CHUNK, (hh + 1) * GLA_CHUNK)
                vh = vpair[:, hh * GLA_DV:(hh + 1) * GLA_DV]
                o = jnp.dot(a2[hr], vh, preferred_element_type=jnp.float32) + inter[hr]
                oc = (2 * p + hh) * GLA_DV
                o_ref[rows, oc:oc + GLA_DV] = o
            k2 = jnp.concatenate([jnp.where(first, kd, 0.0), jnp.where(first, 0.0, kd)],
                                 axis=0).astype(jnp.bfloat16)
            v2 = jnp.concatenate([vpair[:, :GLA_DV], vpair[:, GLA_DV:]], axis=0)
            st_ref[p] = st * jnp.exp(bl) + lax.dot_general(
                v2, k2, _TN, preferred_element_type=jnp.float32)


def _gla_kernel(start_ref, n_ref, zf, gzf, zb, gzb,
                w2f, gbf, w2b, gbb, tril, triu, maskf, maskb,
                of_ref, ob_ref, stf, stb):
    s = pl.program_id(0)
    i = pl.program_id(1)

    @pl.when(i == 0)
    def _():
        stf[...] = jnp.zeros_like(stf)
        stb[...] = jnp.zeros_like(stb)

    @pl.when(i < n_ref[s])
    def _():
        b_f = _log_decay(gzf, w2f, gbf, tril)
        b_b = _log_decay(gzb, w2b, gbb, triu)
        mf = maskf[...]
        mb = maskb[...]
        for step in range(GLA_G // GLA_CHUNK):
            _gla_chunk(zf, b_f, mf, stf, of_ref, False, step)
            _gla_chunk(zb, b_b, mb, stb, ob_ref, True, step)


def _gla(z, seq_start, seq_n, max_groups, w2f, gbf, w2b, gbb):
    t = z.shape[0]
    nseq = seq_start.shape[0]

    def fwd(s, i, st, n):
        return st[s] + jnp.minimum(i, n[s] - 1)

    def bwd(s, i, st, n):
        return st[s] + jnp.maximum(n[s] - 1 - i, 0)

    def zspec(width, col, blk):
        return pl.BlockSpec((GLA_G, width), lambda s, i, st, n: (blk(s, i, st, n), col))

    def const(shape):
        return pl.BlockSpec(shape, lambda s, i, st, n: (0,) * len(shape))

    r = lax.broadcasted_iota(jnp.int32, (TRI_W, TRI_W), 0)
    c = lax.broadcasted_iota(jnp.int32, (TRI_W, TRI_W), 1)
    same = (r // GLA_CHUNK) == (c // GLA_CHUNK)
    tril = (same & (c <= r)).astype(jnp.bfloat16)
    triu = (same & (c >= r)).astype(jnp.bfloat16)
    rr = lax.broadcasted_iota(jnp.int32, (GLA_CHUNK, GLA_CHUNK), 0)
    cc = lax.broadcasted_iota(jnp.int32, (GLA_CHUNK, GLA_CHUNK), 1)
    maskf = (cc <= rr).astype(jnp.float32)
    maskb = (cc > rr).astype(jnp.float32)
    maskf = jnp.concatenate([maskf, maskf], axis=0)
    maskb = jnp.concatenate([maskb, maskb], axis=0)

    in_specs = [
        zspec(QKV_W, 0, fwd), zspec(LANES, ZC_GATE_BLK, fwd),
        zspec(QKV_W, 0, bwd), zspec(LANES, ZC_GATE_BLK, bwd),
        const((LANES, KEY_W)), const((1, KEY_W)), const((LANES, KEY_W)), const((1, KEY_W)),
        const((TRI_W, TRI_W)), const((TRI_W, TRI_W)),
        const((2 * GLA_CHUNK, GLA_CHUNK)), const((2 * GLA_CHUNK, GLA_CHUNK)),
    ]
    out_specs = [
        pl.BlockSpec((GLA_G, HEAD_W), lambda s, i, st, n: (fwd(s, i, st, n), 0)),
        pl.BlockSpec((GLA_G, HEAD_W), lambda s, i, st, n: (bwd(s, i, st, n), 0)),
    ]
    grid_spec = pltpu.PrefetchScalarGridSpec(
        num_scalar_prefetch=2,
        grid=(nseq, max_groups),
        in_specs=in_specs,
        out_specs=out_specs,
        scratch_shapes=[pltpu.VMEM((HEAD_PAIRS, GLA_DV, LANES), jnp.float32),
                        pltpu.VMEM((HEAD_PAIRS, GLA_DV, LANES), jnp.float32)],
    )
    return pl.pallas_call(
        _gla_kernel,
        grid_spec=grid_spec,
        out_shape=[jax.ShapeDtypeStruct((t, HEAD_W), jnp.float32),
                   jax.ShapeDtypeStruct((t, HEAD_W), jnp.float32)],
        compiler_params=_cparams(("arbitrary", "arbitrary")),
        name="gla",
    )(seq_start, seq_n, z, z, z, z, w2f, gbf, w2b, gbb, tril, triu, maskf, maskb)


def _gelu(x):
    return 0.5 * x * (1.0 + lax.erf(x * (2.0 ** -0.5)))


def _mixer_router_kernel(np_tiles, of_ref, ob_ref, r_ref, u_ref, vs_ref, xp_ref, xs_ref,
                         gnw_ref, snw_ref, snb_ref, sgw_ref, sgb_ref, wout_ref, n2_ref,
                         rw_ref, rb_ref, tri_ref, upper_ref,
                         h_ref, xloc_ref, route_ref, ttab_ref, lrow_scr, hn_scr):
    i = pl.program_id(0)

    @pl.when(i == 0)
    def _():
        lrow_scr[...] = jnp.zeros_like(lrow_scr)
        hn_scr[...] = jnp.zeros_like(hn_scr)

    rid = lax.broadcasted_iota(jnp.int32, (PB, TM), 0).astype(jnp.float32).astype(jnp.bfloat16)
    one = jnp.ones((PB, TM), jnp.bfloat16)
    zero = jnp.zeros((PB, TM), jnp.bfloat16)

    def place(blocks):
        for rb in blocks:
            hit = None
            for k in range(TOP_K):
                lp = (lrow_scr[k:k + 1, :] - float(rb * PB)).astype(jnp.bfloat16)
                hk = rid == lp
                hit = hk if hit is None else hit | hk
            xloc_ref[rb * PB:(rb + 1) * PB, :] = jnp.dot(
                jnp.where(hit, one, zero), hn_scr[...],
                preferred_element_type=jnp.float32).astype(jnp.bfloat16)

    place((0,))
    o = of_ref[...] + ob_ref[...]
    parts = []
    for hd in range(GLA_HEADS):
        oh = o[:, hd * GLA_DV:(hd + 1) * GLA_DV]
        ms = jnp.mean(oh * oh, axis=-1, keepdims=True)
        parts.append(oh * lax.rsqrt(ms + EPS))
    on = jnp.concatenate(parts, axis=1) * gnw_ref[...]
    r = r_ref[...].astype(jnp.float32)
    gla_out = on * (r / (1.0 + jnp.exp(-r)))

    place((1,))
    u = _gelu(u_ref[...].astype(jnp.float32))
    place((2,))
    vf = _gelu(vs_ref[...].astype(jnp.float32))
    place((3,))
    parts = []
    for g in range(SG_GROUPS):
        vg = vf[:, g * SG_CH:(g + 1) * SG_CH]
        mu = jnp.mean(vg, axis=-1, keepdims=True)
        vc = vg - mu
        var = jnp.mean(vc * vc, axis=-1, keepdims=True)
        parts.append(vc * lax.rsqrt(var + EPS))
    vn = (jnp.concatenate(parts, axis=1) * snw_ref[...] + snb_ref[...]).astype(jnp.bfloat16)
    nch = TM // SG_CHUNK
    res = []
    for g in range(SG_GROUPS):
        vcat = jnp.concatenate(
            [vn[c * SG_CHUNK:(c + 1) * SG_CHUNK, g * SG_CH:(g + 1) * SG_CH] for c in range(nch)],
            axis=1)
        res.append(jnp.dot(sgw_ref[g], vcat, preferred_element_type=jnp.float32))
    bias = sgb_ref[...]
    rows = []
    for c in range(nch):
        rows.append(jnp.concatenate(
            [res[g][:, c * SG_CH:(c + 1) * SG_CH] for g in range(SG_GROUPS)], axis=1) + bias)
    sg_out = u * jnp.concatenate(rows, axis=0)

    mixed = jnp.concatenate([gla_out, sg_out], axis=1).astype(jnp.bfloat16)
    x = jnp.where(i < np_tiles, xp_ref[...], xs_ref[...])
    h = x + jnp.dot(mixed, wout_ref[...], preferred_element_type=jnp.float32)
    h_ref[...] = h

    ms = jnp.mean(h * h, axis=-1, keepdims=True)
    hn = (h * lax.rsqrt(ms + EPS)) * n2_ref[...]
    hn_hi = hn.astype(jnp.bfloat16)
    hn_lo = (hn - hn_hi.astype(jnp.float32)).astype(jnp.bfloat16)
    lg2 = jnp.dot(hn_hi, rw_ref[...], preferred_element_type=jnp.float32)
    logits = (lg2[:, :LANES] + lg2[:, LANES:]
              + jnp.dot(hn_lo, rw_ref[:, :LANES], preferred_element_type=jnp.float32)
              + rb_ref[...])
    lane = lax.broadcasted_iota(jnp.int32, logits.shape, 1)
    work = jnp.where(lane < N_EXPERTS, logits, NEG)
    top_v, top_i, onehots = [], [], []
    for k in range(TOP_K):
        place((4 + k,))
        m = jnp.max(work, axis=-1, keepdims=True)
        idx = jnp.min(jnp.where(work == m, lane, LANES), axis=-1, keepdims=True)
        sel = lane == idx
        top_v.append(m)
        top_i.append(idx)
        onehots.append(sel)
        work = jnp.where(sel, NEG, work)
    e = [jnp.exp(v - top_v[0]) for v in top_v]
    den = e[0] + e[1] + e[2] + e[3]
    sel_any = onehots[0] | onehots[1] | onehots[2] | onehots[3]
    msel = sel_any.astype(jnp.bfloat16)
    before = jnp.dot(tri_ref[...], msel, preferred_element_type=jnp.float32)
    cnt = jnp.sum(sel_any.astype(jnp.float32), axis=0, keepdims=True)
    nchunk = jnp.floor((cnt + (CH - 1)) * (1.0 / CH))
    nchunk8 = jnp.broadcast_to(nchunk, (8, LANES)).astype(jnp.bfloat16)
    cstart = jnp.dot(nchunk8, upper_ref[...], preferred_element_type=jnp.float32)
    base = before + cstart[0:1, :] * float(CH)
    place(range(4 + TOP_K, R_LOC // PB))
    route = jnp.zeros(logits.shape, jnp.float32)
    for k in range(TOP_K):
        lpos = jnp.sum(jnp.where(onehots[k], base, 0.0), axis=-1, keepdims=True)
        route = jnp.where(lane == k, lpos, route)
        route = jnp.where(lane == TOP_K + k, e[k] / den, route)
    route_ref[...] = route
    sub = lax.broadcasted_iota(jnp.int32, (8, LANES), 0)
    ttab = jnp.where(sub == 0, nchunk8.astype(jnp.float32), jnp.where(sub == 1, cstart, 0.0))
    ttab_ref[...] = ttab.astype(jnp.int32)
    lrow_scr[...] = jnp.transpose(route)[0:8, :]
    hn_scr[...] = hn_hi


def _mixer_router(o_f, o_b, z, xp, xs, gnw, snw, snb, sgw, sgb_full, wout, n2, rw, rb):
    t = z.shape[0]
    npt = xp.shape[0] // TM
    nt = t // TM
    tri = (lax.broadcasted_iota(jnp.int32, (TM, TM), 1)
           < lax.broadcasted_iota(jnp.int32, (TM, TM), 0)).astype(jnp.bfloat16)
    upper = (lax.broadcasted_iota(jnp.int32, (LANES, LANES), 0)
             < lax.broadcasted_iota(jnp.int32, (LANES, LANES), 1)).astype(jnp.bfloat16)

    def const(shape):
        return pl.BlockSpec(shape, lambda i: (0,) * len(shape))

    nst = nt - npt

    def tok(width, col=0):
        return pl.BlockSpec((TM, width), lambda i: (jnp.minimum(i, nt - 1), col))

    in_specs = [
        tok(HEAD_W), tok(HEAD_W),
        tok(HEAD_W, ZC_R), tok(HEAD_W, ZC_U), tok(HEAD_W, ZC_VS),
        pl.BlockSpec((TM, D_MODEL), lambda i: (jnp.minimum(i, npt - 1), 0)),
        pl.BlockSpec((TM, D_MODEL), lambda i: (jnp.clip(i - npt, 0, nst - 1), 0)),
        const((1, HEAD_W)), const((1, HEAD_W)), const((1, HEAD_W)),
        const((SG_GROUPS, SG_CHUNK, SG_CHUNK)), const((SG_CHUNK, HEAD_W)),
        const((D_MODEL, D_MODEL)), const((1, D_MODEL)),
        const((D_MODEL, 2 * LANES)), const((1, LANES)), const((TM, TM)), const((LANES, LANES)),
    ]
    out_specs = [
        tok(D_MODEL),
        pl.BlockSpec((R_LOC, D_MODEL), lambda i: (jnp.maximum(i - 1, 0), 0)),
        tok(LANES),
        pl.BlockSpec((8, LANES), lambda i: (jnp.minimum(i, nt - 1), 0)),
    ]
    out_shape = [
        jax.ShapeDtypeStruct((t, D_MODEL), jnp.float32),
        jax.ShapeDtypeStruct((nt * R_LOC, D_MODEL), jnp.bfloat16),
        jax.ShapeDtypeStruct((t, LANES), jnp.float32),
        jax.ShapeDtypeStruct((nt * 8, LANES), jnp.int32),
    ]
    return pl.pallas_call(
        functools.partial(_mixer_router_kernel, npt),
        grid=(nt + 1,),
        in_specs=in_specs,
        out_specs=out_specs,
        out_shape=out_shape,
        scratch_shapes=[pltpu.VMEM((8, TM), jnp.float32),
                        pltpu.VMEM((TM, D_MODEL), jnp.bfloat16)],
        compiler_params=_cparams(("arbitrary",)),
        name="mixer_router",
    )(o_f, o_b, z, z, z, xp, xs, gnw, snw, snb, sgw, sgb_full, wout, n2, rw, rb, tri, upper)


def _chunk_rows(chunk):
    start = chunk * CH
    return pl.ds(start if isinstance(chunk, int) else pl.multiple_of(start, CH), CH)


def _chunk_copy(src_ref, src_chunk, dst_ref, dst_chunk, sem):
    return pltpu.make_async_copy(src_ref.at[_chunk_rows(src_chunk)],
                                 dst_ref.at[_chunk_rows(dst_chunk)], sem)


def _expert_kernel(tab_ref, te_ref, nv_ref, xloc_ref, w1_ref, b1_ref, w2_ref, b2_ref,
                   yloc_ref, xbuf, ybuf, w1b, w2b, gsem, ssem):
    j = pl.program_id(0)
    last = pl.num_programs(0) - 1
    slot = lax.rem(j, 2)
    nv = nv_ref[j]

    def move(tile, start, chunk_copy, full_copy):
        n = nv_ref[tile]

        @pl.when(n == CPT)
        def _():
            if start:
                for c in range(CPT):
                    chunk_copy(c).start()
            else:
                full_copy().wait()

        @pl.when(n < CPT)
        def _():
            def body(c, carry):
                if start:
                    chunk_copy(c).start()
                else:
                    chunk_copy(c).wait()
                return carry
            lax.fori_loop(0, n, body, 0)

    def gather(tile, buf_slot, start):
        move(tile, start,
             lambda c: _chunk_copy(xloc_ref, tab_ref[tile * CPT + c], xbuf.at[buf_slot], c,
                                   gsem.at[buf_slot]),
             lambda: pltpu.make_async_copy(xloc_ref.at[pl.ds(0, TM_E)], xbuf.at[buf_slot],
                                           gsem.at[buf_slot]))

    def scatter(tile, buf_slot, start):
        move(tile, start,
             lambda c: _chunk_copy(ybuf.at[buf_slot], c, yloc_ref, tab_ref[tile * CPT + c],
                                   ssem.at[buf_slot]),
             lambda: pltpu.make_async_copy(ybuf.at[buf_slot], yloc_ref.at[pl.ds(0, TM_E)],
                                           ssem.at[buf_slot]))

    @pl.when(j == 0)
    def _():
        gather(0, 0, True)

    gather(j, slot, False)

    @pl.when(j < last)
    def _():
        gather(j + 1, 1 - slot, True)

    prev = te_ref[jnp.maximum(j - 1, 0)]

    @pl.when((j == 0) | (te_ref[j] != prev))
    def _():
        w1b[...] = w1_ref[0].astype(jnp.bfloat16)
        w2b[...] = w2_ref[0].astype(jnp.bfloat16)

    @pl.when(j >= 2)
    def _():
        scatter(j - 2, slot, False)

    @pl.when(nv > 0)
    def _():
        row = lax.broadcasted_iota(jnp.int32, (TM_E, 1), 0)
        x = xbuf[slot]
        x = jnp.where(row < nv * CH, x, jnp.zeros_like(x))
        zz = jnp.dot(x, w1b[...], preferred_element_type=jnp.float32) + b1_ref[0]
        gate = jnp.minimum(zz[:, :D_FF], SWIGLU_LIMIT)
        lin = jnp.clip(zz[:, D_FF:], -SWIGLU_LIMIT, SWIGLU_LIMIT)
        a = gate * (1.0 / (1.0 + jnp.exp(-SWIGLU_ALPHA * gate))) * (lin + 1.0)
        y = jnp.dot(a.astype(jnp.bfloat16), w2b[...],
                    preferred_element_type=jnp.float32) + b2_ref[0]
        ybuf[slot] = y.astype(jnp.bfloat16)
        scatter(j, slot, True)

    @pl.when(j == last)
    def _():
        @pl.when(j >= 1)
        def _():
            scatter(j - 1, 1 - slot, False)
        scatter(j, slot, False)


def _experts(x_loc, tab, te, nv, w1, b1, w2, b2):
    ntiles = te.shape[0]
    grid_spec = pltpu.PrefetchScalarGridSpec(
        num_scalar_prefetch=3,
        grid=(ntiles,),
        in_specs=[
            pl.BlockSpec(memory_space=pl.ANY),
            pl.BlockSpec((1, D_MODEL, 2 * D_FF), lambda j, tab, te, nv: (te[j], 0, 0)),
            pl.BlockSpec((1, 1, 2 * D_FF), lambda j, tab, te, nv: (te[j], 0, 0)),
            pl.BlockSpec((1, D_FF, D_MODEL), lambda j, tab, te, nv: (te[j], 0, 0)),
            pl.BlockSpec((1, 1, D_MODEL), lambda j, tab, te, nv: (te[j], 0, 0)),
        ],
        out_specs=pl.BlockSpec(memory_space=pl.ANY),
        scratch_shapes=[pltpu.VMEM((2, TM_E, D_MODEL), jnp.bfloat16),
                        pltpu.VMEM((2, TM_E, D_MODEL), jnp.bfloat16),
                        pltpu.VMEM((D_MODEL, 2 * D_FF), jnp.bfloat16),
                        pltpu.VMEM((D_FF, D_MODEL), jnp.bfloat16),
                        pltpu.SemaphoreType.DMA((2,)),
                        pltpu.SemaphoreType.DMA((2,))],
    )
    return pl.pallas_call(
        _expert_kernel,
        grid_spec=grid_spec,
        out_shape=jax.ShapeDtypeStruct(x_loc.shape, jnp.bfloat16),
        input_output_aliases={3: 0},
        compiler_params=_cparams(("arbitrary",)),
        name="experts",
    )(tab, te, nv, x_loc, w1, b1, w2, b2)


def _combine_kernel(np_tiles, h_ref, route_ref, fw_ref, yloc_ref, outp_ref, outs_ref):
    i = pl.program_id(0)
    route = route_ref[...]
    cid = lax.broadcasted_iota(jnp.int32, (TM, LANES), 1).astype(jnp.float32)
    zero = jnp.zeros((TM, LANES), jnp.bfloat16)
    lpos = [jnp.broadcast_to(route[:, k:k + 1], (TM, LANES)) for k in range(TOP_K)]
    gates = [jnp.broadcast_to(route[:, TOP_K + k:TOP_K + k + 1], (TM, LANES)).astype(jnp.bfloat16)
             for k in range(TOP_K)]
    acc = h_ref[...]
    for cb in range(R_LOC // PB):
        lps = [(lpos[k] - float(cb * PB)).astype(jnp.bfloat16) for k in range(TOP_K)]
        tiles = []
        for lt in range(PB // LANES):
            cl = (cid + float(lt * LANES)).astype(jnp.bfloat16)
            w = None
            for k in range(TOP_K):
                wk = jnp.where(cl == lps[k], gates[k], zero)
                w = wk if w is None else w + wk
            tiles.append(w)
        acc = acc + jnp.dot(jnp.concatenate(tiles, axis=1), yloc_ref[cb * PB:(cb + 1) * PB, :],
                            preferred_element_type=jnp.float32)
    ms = jnp.mean(acc * acc, axis=-1, keepdims=True)
    res = (acc * lax.rsqrt(ms + EPS)) * fw_ref[...]

    @pl.when(i < np_tiles)
    def _():
        outp_ref[...] = res

    @pl.when(i >= np_tiles)
    def _():
        outs_ref[...] = res


def _combine(h, route, y_loc, fw, tp, ts):
    t = h.shape[0]
    npt = tp // TM
    nt = t // TM
    return pl.pallas_call(
        functools.partial(_combine_kernel, npt),
        grid=(nt,),
        in_specs=[
            pl.BlockSpec((TM, D_MODEL), lambda i: (i, 0)),
            pl.BlockSpec((TM, LANES), lambda i: (i, 0)),
            pl.BlockSpec((1, D_MODEL), lambda i: (0, 0)),
            pl.BlockSpec((R_LOC, D_MODEL), lambda i: (i, 0)),
        ],
        out_specs=[
            pl.BlockSpec((TM, D_MODEL), lambda i: (jnp.minimum(i, npt - 1), 0)),
            pl.BlockSpec((TM, D_MODEL), lambda i: (jnp.maximum(i - npt, 0), 0)),
        ],
        out_shape=[jax.ShapeDtypeStruct((tp, D_MODEL), jnp.float32),
                   jax.ShapeDtypeStruct((ts, D_MODEL), jnp.float32)],
        compiler_params=_cparams(("arbitrary",)),
        name="combine",
    )(h, route, fw, y_loc)


def _forward(x_prompt, x_sample, norm1_w, w_in, gla_w2_f, gla_b_f, gla_w2_b, gla_b_b,
             gla_norm_w, sg_norm_w, sg_norm_b, sg_w, sg_b, w_out, norm2_w, router_w,
             router_b, exp_w1, exp_b1, exp_w2, exp_b2, final_norm_w):
    bp, lp, _ = x_prompt.shape
    bs, ls, _ = x_sample.shape
    tp, ts = bp * lp, bs * ls
    t = tp + ts
    assert lp % TM == 0 and ls % TM == 0 and TM % GLA_G == 0
    xp = x_prompt.reshape(tp, D_MODEL)
    xs = x_sample.reshape(ts, D_MODEL)

    g0 = QKV_W + HEAD_W
    u0 = g0 + 2 * GATE_RANK
    wgate = jnp.pad(w_in[:, g0:u0], ((0, 0), (0, LANES - 2 * GATE_RANK)))
    w_all = jnp.concatenate(
        [w_in[:, :KEY_W] * (GLA_DK ** -0.5), w_in[:, KEY_W:g0], w_in[:, u0:], wgate], axis=1
    ).astype(jnp.bfloat16)

    z = _inproj(xp, xs, norm1_w.reshape(1, D_MODEL), w_all)

    w2f = jnp.zeros((LANES, KEY_W), jnp.float32).at[:GATE_RANK].set(gla_w2_f)
    w2b = jnp.zeros((LANES, KEY_W), jnp.float32).at[GATE_RANK:2 * GATE_RANK].set(gla_w2_b)
    gbf = gla_b_f.reshape(1, -1)
    gbb = gla_b_b.reshape(1, -1)
    gp, gs = lp // GLA_G, ls // GLA_G
    seq_start = jnp.concatenate([jnp.arange(bp, dtype=jnp.int32) * gp,
                                 bp * gp + jnp.arange(bs, dtype=jnp.int32) * gs])
    seq_n = jnp.concatenate([jnp.full((bp,), gp, jnp.int32), jnp.full((bs,), gs, jnp.int32)])
    o_f, o_b = _gla(z, seq_start, seq_n, max(gp, gs), w2f.astype(jnp.bfloat16), gbf,
                    w2b.astype(jnp.bfloat16), gbb)

    sgb_full = jnp.repeat(sg_b.T, SG_CH, axis=1)
    rw = jnp.pad(router_w, ((0, 0), (0, LANES - N_EXPERTS)))
    rw_hi = rw.astype(jnp.bfloat16)
    rw_lo = (rw - rw_hi.astype(jnp.float32)).astype(jnp.bfloat16)
    rw = jnp.concatenate([rw_hi, rw_lo], axis=1)
    rb = jnp.pad(router_b.reshape(1, -1), ((0, 0), (0, LANES - N_EXPERTS)))
    h, x_loc, route, ttab = _mixer_router(
        o_f, o_b, z, xp, xs, gla_norm_w.reshape(1, -1), sg_norm_w.reshape(1, -1),
        sg_norm_b.reshape(1, -1), sg_w.astype(jnp.bfloat16), sgb_full,
        w_out.astype(jnp.bfloat16), norm2_w.reshape(1, -1), rw, rb)

    nt = t // TM
    ttab = ttab.reshape(nt, 8, LANES)
    nch = ttab[:, 0, :N_EXPERTS]
    cst = ttab[:, 1, :N_EXPERTS]
    tot = jnp.sum(nch, axis=0)
    padded = ((tot + CPT - 1) // CPT) * CPT
    gend = jnp.cumsum(padded)
    gstart = gend - padded
    seg_len = jnp.concatenate([nch.T, (padded - tot)[:, None]], axis=1).reshape(-1)
    seg_dst = jnp.cumsum(seg_len) - seg_len
    src = jnp.arange(nt, dtype=jnp.int32)[:, None] * CPL + cst
    seg_src = jnp.concatenate([src.T, jnp.zeros((N_EXPERTS, 1), jnp.int32)], axis=1).reshape(-1)
    ntiles = (TOP_K * t // CH + nt * N_EXPERTS + CPT - 1) // CPT + N_EXPERTS
    d = jnp.arange(ntiles * CPT, dtype=jnp.int32)
    nseg = nt + 1
    off = seg_src - seg_dst
    delta = (off - jnp.concatenate([jnp.zeros((1,), off.dtype), off[:-1]])).astype(jnp.float32)
    delta = delta.reshape(N_EXPERTS, nseg)
    region_base = jnp.cumsum(jnp.sum(delta, axis=1)) - jnp.sum(delta, axis=1)
    e_of_d = jnp.minimum(jnp.sum((gend[None, :] <= d[:, None]).astype(jnp.int32), axis=1),
                         N_EXPERTS - 1)
    pick = jax.nn.one_hot(e_of_d, N_EXPERTS, dtype=jnp.float32)
    hp = lax.Precision.HIGHEST
    starts_d = jnp.dot(pick, seg_dst.reshape(N_EXPERTS, nseg).astype(jnp.float32), precision=hp)
    delta_d = jnp.dot(pick, delta, precision=hp)
    base_d = jnp.dot(pick, region_base, precision=hp)
    df = d.astype(jnp.float32)
    tab = df + base_d + jnp.sum(jnp.where(starts_d <= df[:, None], delta_d, 0.0), axis=1)
    tab = tab.astype(jnp.int32)
    cstart_tile = jnp.arange(ntiles, dtype=jnp.int32) * CPT
    te = jnp.sum((gend[None, :] <= cstart_tile[:, None]).astype(jnp.int32), axis=1)
    te = jnp.minimum(te, N_EXPERTS - 1)
    nv = jnp.clip(gstart[te] + tot[te] - cstart_tile, 0, CPT)
    nv = jnp.where(cstart_tile < gend[-1], nv, 0).astype(jnp.int32)
    te = jnp.where(cstart_tile < gend[-1], te, te[jnp.maximum(gend[-1] // CPT - 1, 0)])
    te = te.astype(jnp.int32)

    y_loc = _experts(x_loc, tab, te, nv, exp_w1, exp_b1.reshape(N_EXPERTS, 1, 2 * D_FF),
                     exp_w2, exp_b2.reshape(N_EXPERTS, 1, D_MODEL))
    y_p, y_s = _combine(h, route, y_loc, final_norm_w.reshape(1, -1), tp, ts)
    return y_p.reshape(bp, lp, D_MODEL), y_s.reshape(bs, ls, D_MODEL)


def kernel(x_prompt, x_sample, norm1_w, w_in, gla_w2_f, gla_b_f, gla_w2_b, gla_b_b, gla_norm_w, sg_norm_w, sg_norm_b, sg_w, sg_b, w_out, norm2_w, router_w, router_b, exp_w1, exp_b1, exp_w2, exp_b2, final_norm_w):
    return _forward(x_prompt, x_sample, norm1_w, w_in, gla_w2_f, gla_b_f, gla_w2_b, gla_b_b,
                    gla_norm_w, sg_norm_w, sg_norm_b, sg_w, sg_b, w_out, norm2_w, router_w,
                    router_b, exp_w1, exp_b1, exp_w2, exp_b2, final_norm_w)
```
